```python
import math
import jax, jax.numpy as jnp
from jax import lax
import numpy as np

D_MODEL = 1024
BATCH = 8
SEQ = 4096
DEPTH = 4

HEAD_DIM = 64
D_ATTN = D_MODEL // 2
D_RWKV = D_MODEL - D_ATTN
D_MIX = D_ATTN + D_RWKV
N_ATTN_HEADS = D_ATTN // HEAD_DIM
N_RWKV_HEADS = D_RWKV // HEAD_DIM
DILATED_BRANCHES = ((128, 1), (512, 4), (2048, 16))
Q_BLOCK = 128
LORA_DECAY = 32
LORA_AAA = 32
LORA_MV = 32
LORA_GATE = 96
D_FF = 4 * D_MODEL
NORM_EPS = 1e-6
GN_EPS = 64e-5
N_SHIFT = 3 * D_RWKV + LORA_DECAY + LORA_AAA + LORA_GATE
N_COLS_FIRST = 3 * D_ATTN + N_SHIFT
N_COLS_REST = N_COLS_FIRST + LORA_MV

kernel_name = "hybrid_dilated_attn_rwkv7_sandwich"


def _rms_norm(x, g):
    xf = x.astype(jnp.float32)
    y = xf * lax.rsqrt(jnp.mean(xf * xf, axis=-1, keepdims=True) + NORM_EPS)
    return (y * g.astype(jnp.float32)).astype(x.dtype)


def _token_shift(z, mu):
    prev = jnp.pad(z, ((0, 0), (1, 0), (0, 0)))[:, :-1]
    return z + (prev - z) * mu


def _dilated_branch(q, k, v, window, dilation):
    B, S, H, Dh = q.shape
    L = S // dilation
    K = window // dilation
    qb = math.gcd(L, Q_BLOCK)
    nb = L // qb

    def phase(t):
        return t.reshape(B, L, dilation, H, Dh).transpose(0, 2, 3, 1, 4)

    qs, ks, vs = phase(q), phase(k), phase(v)
    pad = ((0, 0), (0, 0), (0, 0), (K, 0), (0, 0))
    kp, vp = jnp.pad(ks, pad), jnp.pad(vs, pad)
    idx = jnp.arange(nb)[:, None] * qb + jnp.arange(qb + K)[None, :]
    kb = kp[:, :, :, idx, :]
    vb = vp[:, :, :, idx, :]
    qblk = qs.reshape(B, dilation, H, nb, qb, Dh)
    s = jnp.einsum('bdhnqc,bdhnkc->bdhnqk', qblk, kb) * (1.0 / math.sqrt(Dh))
    j = jnp.arange(qb)[:, None]
    m_idx = jnp.arange(qb + K)[None, :]
    dist = j + K - m_idx
    blk = jnp.arange(nb)[:, None, None]
    valid = (dist >= 0) & (dist <= K) & (blk * qb + m_idx - K >= 0)
    s = jnp.where(valid, s, -jnp.inf)
    m = jnp.max(s, axis=-1)
    p = jnp.exp(s - m[..., None])
    l = jnp.sum(p, axis=-1)
    acc = jnp.einsum('bdhnqk,bdhnkc->bdhnqc', p, vb)
    acc = acc.reshape(B, dilation, H, L, Dh).transpose(0, 3, 1, 2, 4).reshape(B, S, H, Dh)
    m = m.reshape(B, dilation, H, L).transpose(0, 3, 1, 2).reshape(B, S, H)
    l = l.reshape(B, dilation, H, L).transpose(0, 3, 1, 2).reshape(B, S, H)
    return m, l, acc


def _dilated_attention(q, k, v):
    B, S, _ = q.shape
    heads = lambda t: t.astype(jnp.float32).reshape(B, S, N_ATTN_HEADS, HEAD_DIM)
    qh, kh, vh = heads(q), heads(k), heads(v)
    outs = [_dilated_branch(qh, kh, vh, w, d) for (w, d) in DILATED_BRANCHES]
    m_all = jnp.maximum(jnp.maximum(outs[0][0], outs[1][0]), outs[2][0])
    num = sum(jnp.exp(m - m_all)[..., None] * acc for (m, l, acc) in outs)
    den = sum(jnp.exp(m - m_all) * l for (m, l, acc) in outs)
    o = num / den[..., None]
    return o.reshape(B, S, D_ATTN).astype(q.dtype)


def _rwkv7_scan(r, w, k, v, kk, b):
    B, S, H, N = r.shape

    def step(state, inp):
        r_t, w_t, k_t, v_t, kk_t, b_t = inp
        sa = jnp.einsum('bhij,bhj->bhi', state, kk_t)
        state = (state * w_t[:, :, None, :]
                 - sa[..., None] * b_t[:, :, None, :]
                 + v_t[..., None] * k_t[:, :, None, :])
        y = jnp.einsum('bhij,bhj->bhi', state, r_t)
        return state, y

    xs = tuple(jnp.moveaxis(t, 1, 0) for t in (r, w, k, v, kk, b))
    state0 = jnp.zeros((B, H, N, N), jnp.float32)
    _, ys = lax.scan(step, state0, xs)
    return jnp.moveaxis(ys, 0, 1)


def _rwkv7(r, k, v, xw, xa, xg, w0, w_up, a0, a_up, g_up, k_k, k_a, r_k, gn_w, gn_b):
    B, S, _ = r.shape
    H, N = N_RWKV_HEADS, HEAD_DIM
    f32 = jnp.float32
    heads = lambda t: t.astype(f32).reshape(B, S, H, N)
    logw = -jax.nn.softplus(-(w0 + jnp.tanh(xw) @ w_up).astype(f32)) - 0.5
    decay = jnp.exp(-jnp.exp(logw))
    a = jax.nn.sigmoid((a0 + xa @ a_up).astype(f32))
    g = (jax.nn.sigmoid(xg) @ g_up).astype(f32)
    kk = heads(k * k_k)
    kk = kk / jnp.maximum(jnp.sqrt(jnp.sum(kk * kk, axis=-1, keepdims=True)), 1e-12)
    k_mod = k.astype(f32) * (1.0 + (a - 1.0) * k_a.astype(f32))
    rh, kh, vh, ah, wh = heads(r), heads(k_mod), heads(v), heads(a), heads(decay)
    y = _rwkv7_scan(rh, wh, kh, vh, kk, kk * ah)
    mean = jnp.mean(y, axis=-1, keepdims=True)
    var = jnp.mean(jnp.square(y - mean), axis=-1, keepdims=True)
    yn = ((y - mean) * lax.rsqrt(var + GN_EPS)).reshape(B, S, D_RWKV)
    yn = yn * gn_w.astype(f32) + gn_b.astype(f32)
    bonus = jnp.sum(rh * kh * r_k.astype(f32), axis=-1, keepdims=True) * vh
    out = (yn + bonus.reshape(B, S, D_RWKV)) * g
    return out.astype(r.dtype)


def setup_inputs(seed: int = 0) -> dict:
    key = jax.random.key(seed)
    ks = iter(jax.random.split(key, 40))
    f32 = jnp.float32
    nrm = lambda shape, scale: jax.random.normal(next(ks), shape, f32) * scale
    gain = lambda shape: 1.0 + nrm(shape, 0.05)
    L1 = DEPTH - 1
    return {
        "x": nrm((BATCH, SEQ, D_MODEL), 1.0),
        "norm_mix_pre": gain((DEPTH, D_MODEL)),
        "norm_mix_post": gain((DEPTH, D_MODEL)),
        "norm_ffn_pre": gain((DEPTH, D_MODEL)),
        "norm_ffn_post": gain((DEPTH, D_MODEL)),
        "w_in_first": nrm((D_MODEL, N_COLS_FIRST), D_MODEL ** -0.5),
        "w_in_rest": nrm((L1, D_MODEL, N_COLS_REST), D_MODEL ** -0.5),
        "mu_shift": jax.random.uniform(next(ks), (DEPTH, N_SHIFT), f32),
        "mu_shift_mv": jax.random.uniform(next(ks), (L1, LORA_MV), f32),
        "attn_out_gain": gain((DEPTH, D_ATTN)),
        "decay_w0": jax.random.uniform(next(ks), (DEPTH, D_RWKV), f32, -5.0, 0.0),
        "decay_up": nrm((DEPTH, LORA_DECAY, D_RWKV), 0.5 * LORA_DECAY ** -0.5),
        "aaa_a0": nrm((DEPTH, D_RWKV), 0.1),
        "aaa_up": nrm((DEPTH, LORA_AAA, D_RWKV), 0.5 * LORA_AAA ** -0.5),
        "mv_v0": nrm((L1, D_RWKV), 0.5),
        "mv_up": nrm((L1, LORA_MV, D_RWKV), 0.5 * LORA_MV ** -0.5),
        "gate_up": nrm((DEPTH, LORA_GATE, D_RWKV), LORA_GATE ** -0.5),
        "k_k": 0.85 + nrm((DEPTH, D_RWKV), 0.1),
        "k_a": 1.0 + nrm((DEPTH, D_RWKV), 0.1),
        "r_k": nrm((DEPTH, N_RWKV_HEADS, HEAD_DIM), 0.1),
        "gn_w": gain((DEPTH, D_RWKV)),
        "gn_b": nrm((DEPTH, D_RWKV), 0.02),
        "w_out": nrm((DEPTH, D_MIX, D_MODEL), D_MIX ** -0.5),
        "w_ffn_up": nrm((DEPTH, D_MODEL, D_FF), D_MODEL ** -0.5),
        "w_ffn_down": nrm((DEPTH, D_FF, D_MODEL), D_FF ** -0.5),
    }


def reference(x, norm_mix_pre, norm_mix_post, norm_ffn_pre, norm_ffn_post,
              w_in_first, w_in_rest, mu_shift, mu_shift_mv, attn_out_gain,
              decay_w0, decay_up, aaa_a0, aaa_up, mv_v0, mv_up, gate_up,
              k_k, k_a, r_k, gn_w, gn_b, w_out, w_ffn_up, w_ffn_down):
    v_first = None
    o0, o1, o2 = 3 * D_RWKV, 3 * D_RWKV + LORA_DECAY, 3 * D_RWKV + LORA_DECAY + LORA_AAA
    for i in range(DEPTH):
        h = _rms_norm(x, norm_mix_pre[i])
        z = h @ (w_in_first if i == 0 else w_in_rest[i - 1])
        q = z[..., 0:D_ATTN]
        k = z[..., D_ATTN:2 * D_ATTN]
        v = z[..., 2 * D_ATTN:3 * D_ATTN]
        zs = _token_shift(z[..., 3 * D_ATTN:3 * D_ATTN + N_SHIFT], mu_shift[i])
        r_r = zs[..., 0:D_RWKV]
        k_r = zs[..., D_RWKV:2 * D_RWKV]
        v_r = zs[..., 2 * D_RWKV:3 * D_RWKV]
        xw = zs[..., o0:o1]
        xa = zs[..., o1:o2]
        xg = zs[..., o2:N_SHIFT]
        if i == 0:
            v_first = v_r
        else:
            xmv = _token_shift(z[..., N_COLS_FIRST:], mu_shift_mv[i - 1])
            vgate = jax.nn.sigmoid(mv_v0[i - 1] + xmv @ mv_up[i - 1])
            v_r = v_r + (v_first - v_r) * vgate
        attn = _rms_norm(_dilated_attention(q, k, v), attn_out_gain[i])
        rw = _rwkv7(r_r, k_r, v_r, xw, xa, xg, decay_w0[i], decay_up[i], aaa_a0[i],
                    aaa_up[i], gate_up[i], k_k[i], k_a[i], r_k[i], gn_w[i], gn_b[i])
        mixed = jnp.concatenate([attn, rw], axis=-1) @ w_out[i]
        x = x + _rms_norm(mixed, norm_mix_post[i])
        h = _rms_norm(x, norm_ffn_pre[i])
        f = jnp.square(jax.nn.relu(h @ w_ffn_up[i])) @ w_ffn_down[i]
        x = x + _rms_norm(f, norm_ffn_post[i])
    return x
```

```python
import functools
import math

import jax
import jax.numpy as jnp
from jax import lax
from jax.experimental import pallas as pl
from jax.experimental.pallas import tpu as pltpu

F32 = jnp.float32
BF16 = jnp.bfloat16

HEAD_DIM = 64
LANES = 128
NORM_EPS = 1e-6
GN_EPS = 64e-5
DILATIONS = (1, 4, 16)
BAND = 128
LORA_DECAY, LORA_AAA, LORA_GATE, LORA_MV = 32, 32, 96, 32
LORA_PAD = 256
CHUNK = 64
NEG_BIG = -1e30
VMEM_LIMIT = 56 * 1024 * 1024


def _nt(a, b):
    return lax.dot_general(a, b, (((1,), (1,)), ((), ())), preferred_element_type=F32)


def _tn(a, b):
    return lax.dot_general(a, b, (((0,), (0,)), ((), ())), preferred_element_type=F32)


def _mm(a, b):
    return jnp.dot(a, b, preferred_element_type=F32)


def _split2(x):
    hi = x.astype(BF16)
    lo = (x - hi.astype(F32)).astype(BF16)
    return hi, lo


def _rms(x, g):
    return x * lax.rsqrt(jnp.mean(x * x, axis=-1, keepdims=True) + NORM_EPS) * g


def _inproj_kernel(x_ref, g_ref, w_ref, mu_ref, za_ref, zr_ref, zl_ref, carry_ref,
                   *, seq_tiles, d_attn3, d_rwkv3):
    i = pl.program_id(0)
    tm = x_ref.shape[0]
    h = _rms(x_ref[...], g_ref[...]).astype(BF16)
    cw = 512
    for c in range(d_attn3 // cw):
        z = _mm(h, w_ref[:, c * cw:(c + 1) * cw])
        za_ref[:, c * cw:(c + 1) * cw] = z.astype(BF16)
    first_tile = (i % seq_tiles) == 0
    row0 = lax.broadcasted_iota(jnp.int32, (tm, 1), 0) == 0
    n_shift = d_rwkv3 + LORA_PAD
    c0 = 0
    while c0 < n_shift:
        w = min(cw, n_shift - c0)
        z = _mm(h, w_ref[:, d_attn3 + c0:d_attn3 + c0 + w])
        carry = jnp.where(first_tile, 0.0, carry_ref[:, c0:c0 + w])
        prev = jnp.where(row0, carry, pltpu.roll(z, 1, 0))
        carry_ref[:, c0:c0 + w] = z[tm - 1:tm, :]
        zs = z + (prev - z) * mu_ref[:, c0:c0 + w]
        if c0 < d_rwkv3:
            zr_ref[:, c0:c0 + w] = zs
        else:
            zl_ref[...] = zs
        c0 += w


def _inproj(x2, g, w, mu, *, seq, tm=512):
    t, d = x2.shape
    nc = w.shape[1]
    d_attn3 = 3 * (d // 2)
    d_rwkv3 = nc - d_attn3 - LORA_PAD
    kern = functools.partial(_inproj_kernel, seq_tiles=seq // tm, d_attn3=d_attn3, d_rwkv3=d_rwkv3)
    return pl.pallas_call(
        kern,
        grid=(t // tm,),
        in_specs=[
            pl.BlockSpec((tm, d), lambda i: (i, 0)),
            pl.BlockSpec((1, d), lambda i: (0, 0)),
            pl.BlockSpec((d, nc), lambda i: (0, 0)),
            pl.BlockSpec((1, nc - d_attn3), lambda i: (0, 0)),
        ],
        out_specs=[
            pl.BlockSpec((tm, d_attn3), lambda i: (i, 0)),
            pl.BlockSpec((tm, d_rwkv3), lambda i: (i, 0)),
            pl.BlockSpec((tm, LORA_PAD), lambda i: (i, 0)),
        ],
        out_shape=[
            jax.ShapeDtypeStruct((t, d_attn3), BF16),
            jax.ShapeDtypeStruct((t, d_rwkv3), F32),
            jax.ShapeDtypeStruct((t, LORA_PAD), F32),
        ],
        scratch_shapes=[pltpu.VMEM((1, nc - d_attn3), F32)],
        compiler_params=pltpu.CompilerParams(
            dimension_semantics=("arbitrary",), vmem_limit_bytes=VMEM_LIMIT),
        name="inproj",
    )(x2, g, w, mu)


def _attn_kernel(q_ref, k_ref, v_ref, o_ref, lse_ref):
    sub_len = q_ref.shape[0]
    lane = lax.broadcasted_iota(jnp.int32, (1, LANES), 1)
    head0 = lane < HEAD_DIM
    qrow = lax.broadcasted_iota(jnp.int32, (BAND, 1), 0)
    kcol = lax.broadcasted_iota(jnp.int32, (1, 2 * BAND), 1)
    scale = 1.0 / math.sqrt(HEAD_DIM)

    def body(i, carry):
        qs = pl.multiple_of(i * BAND, BAND)
        ws = pl.multiple_of(jnp.maximum(i * BAND - BAND, 0), BAND)
        q = q_ref[pl.ds(qs, BAND), :]
        kw = k_ref[pl.ds(ws, 2 * BAND), :]
        vw = v_ref[pl.ds(ws, 2 * BAND), :]
        rel = (qs - ws) + qrow - kcol
        valid = (rel >= 0) & (rel <= BAND)
        outs, lses = [], []
        for hsel in (head0, jnp.logical_not(head0)):
            qh = jnp.where(hsel, q, jnp.zeros_like(q))
            s = _nt(qh, kw) * scale
            s = jnp.where(valid, s, NEG_BIG)
            m = jnp.max(s, axis=-1, keepdims=True)
            p = jnp.exp(s - m)
            l = jnp.sum(p, axis=-1, keepdims=True)
            pv = _mm(p.astype(BF16), vw)
            outs.append(pv / l)
            lses.append(m + jnp.log(l))
        o_ref[pl.ds(qs, BAND), :] = jnp.where(head0, outs[0], outs[1]).astype(o_ref.dtype)
        lse_ref[pl.ds(qs, BAND), :] = jnp.where(head0, lses[0], lses[1])
        return carry

    lax.fori_loop(0, sub_len // BAND, body, 0)


def _attn_branch(za, *, batch, seq, d_attn, dil):
    sub_len = seq // dil
    pairs = d_attn // LANES
    cols = 3 * pairs
    zv = za.reshape(batch, sub_len, dil * 3 * d_attn)
    blk = (None, sub_len, LANES)
    o, lse = pl.pallas_call(
        _attn_kernel,
        grid=(batch, dil, pairs),
        in_specs=[
            pl.BlockSpec(blk, lambda b, r, p: (b, 0, r * cols + p)),
            pl.BlockSpec(blk, lambda b, r, p: (b, 0, r * cols + pairs + p)),
            pl.BlockSpec(blk, lambda b, r, p: (b, 0, r * cols + 2 * pairs + p)),
        ],
        out_specs=[
            pl.BlockSpec(blk, lambda b, r, p: (b, 0, r * pairs + p)),
            pl.BlockSpec(blk, lambda b, r, p: (b, 0, r * pairs + p)),
        ],
        out_shape=[
            jax.ShapeDtypeStruct((batch, sub_len, dil * d_attn), BF16),
            jax.ShapeDtypeStruct((batch, sub_len, dil * d_attn), F32),
        ],
        compiler_params=pltpu.CompilerParams(
            dimension_semantics=("arbitrary", "arbitrary", "arbitrary"),
            vmem_limit_bytes=VMEM_LIMIT),
        name=f"attn_d{dil}",
    )(zv, zv, zv)
    return o.reshape(batch * seq, d_attn), lse.reshape(batch * seq, d_attn)


def _block_diag(x, head0):
    zero = jnp.zeros_like(x)
    return jnp.concatenate([jnp.where(head0, x, zero), jnp.where(head0, zero, x)], axis=0)


def _chunk_consts():
    c = CHUNK
    trow = lax.broadcasted_iota(jnp.int32, (c, 1), 0)
    col3 = lax.broadcasted_iota(jnp.int32, (1, 3 * c), 1)
    tri3 = jnp.where((col3 & (c - 1)) <= trow, 1.0, 0.0).astype(BF16)
    lane_w = lax.broadcasted_iota(jnp.int32, (1, LANES), 1)
    scol = lane_w & (HEAD_DIM - 1)
    strict = scol < trow
    incl = scol <= trow
    eye_w = jnp.where(scol == trow, 1.0, 0.0).astype(F32)
    head0 = lane_w < HEAD_DIM
    rr = lax.broadcasted_iota(jnp.int32, (LANES, 1), 0)
    same_head = (rr // HEAD_DIM) == (lane_w // HEAD_DIM)
    levels = []
    s = 1
    while s < c:
        in_pair = (trow // (2 * s)) == (scol // (2 * s))
        levels.append(in_pair & ((trow // s) != (scol // s)) & strict)
        s *= 2
    return dict(tri3=tri3, strict=strict, incl=incl, eye_w=eye_w, head0=head0,
                same_head=same_head, levels=levels)


def _rwkv_chunk_pair(r, lw, k, v, kk, b, state, consts):
    head0 = consts["head0"]
    strict, incl = consts["strict"], consts["incl"]
    c = r.shape[0]
    hi = lw.astype(BF16)
    r1 = lw - hi.astype(F32)
    mid = r1.astype(BF16)
    lo = (r1 - mid.astype(F32)).astype(BF16)
    cum = _mm(consts["tri3"], jnp.concatenate([hi, mid, lo], axis=0))
    cum_end = cum[c - 1:c, :]
    e_in = jnp.exp(cum)
    e_ex = jnp.exp(cum - lw)
    e_neg = jnp.exp(-cum)
    e_end = jnp.exp(cum_end - cum)
    kq = kk * e_ex
    rq = r * e_in
    kt = k * e_neg
    bt = b * e_neg
    kh = k * e_end
    bh = b * e_end
    lhs = jnp.concatenate([kq, rq], axis=0).astype(BF16)
    rhs = jnp.concatenate([_block_diag(bt, head0), _block_diag(kt, head0)], axis=0).astype(BF16)
    a_all = _nt(lhs, rhs)
    zero = jnp.zeros((c, LANES), F32)
    a_ab = jnp.where(strict, a_all[:c, :LANES], zero)
    a_ak = jnp.where(strict, a_all[:c, LANES:], zero)
    a_rb = jnp.where(incl, a_all[c:, :LANES], zero)
    a_rk = jnp.where(incl, a_all[c:, LANES:], zero)
    levels = consts["levels"]
    t_inv = consts["eye_w"] - jnp.where(levels[0], a_ab, zero)
    for mask in levels[1:]:
        x = _mm(jnp.where(mask, a_ab, zero).astype(BF16), _block_diag(t_inv, head0).astype(BF16))
        t_inv = t_inv - _mm(t_inv.astype(BF16), _block_diag(x, head0).astype(BF16))
    ks = _nt(lhs, state.astype(BF16))
    av = _mm(jnp.concatenate([a_ak, a_rk], axis=0).astype(BF16), _block_diag(v, head0).astype(BF16))
    u = _mm(t_inv.astype(BF16), _block_diag(ks[:c] + av[:c], head0).astype(BF16))
    y = ks[c:] + av[c:] - _mm(a_rb.astype(BF16), _block_diag(u, head0).astype(BF16))
    upd = _tn(jnp.concatenate([v, -u], axis=0).astype(BF16),
              jnp.concatenate([kh, bh], axis=0).astype(BF16))
    new_state = state * jnp.exp(cum_end) + jnp.where(consts["same_head"], upd, jnp.zeros_like(upd))
    return y, new_state


def _rwkv_kernel(*refs, first_layer, tiles_per_seq, d_rwkv):
    if first_layer:
        (zr_ref, zl_ref, wl_ref, vec_ref, hs_ref, out_ref, vfirst_out_ref,
         r_s, lw_s, k_s, v_s, kk_s, b_s, y_s, state_s) = refs
        vfirst_ref = None
    else:
        (zr_ref, zl_ref, vfirst_ref, wl_ref, vec_ref, hs_ref, out_ref,
         r_s, lw_s, k_s, v_s, kk_s, b_s, y_s, state_s) = refs
    tc = zr_ref.shape[0]
    n_pairs = d_rwkv // LANES
    c = CHUNK

    @pl.when(pl.program_id(1) == 0)
    def _():
        state_s[...] = jnp.zeros_like(state_s)

    w0, a0, mv0, k_k, k_a, r_k, gn_w, gn_b = (vec_ref[j:j + 1, :] for j in range(8))
    hs = hs_ref[...]

    def head_sum(x):
        hi, lo = _split2(x)
        return _mm(hi, hs) + _mm(lo, hs)

    zl = zl_ref[...]
    lane = lax.broadcasted_iota(jnp.int32, (1, LORA_PAD), 1)
    o1 = LORA_DECAY
    o2 = o1 + LORA_AAA
    o3 = o2 + LORA_GATE
    act = jnp.where(lane < o1, jnp.tanh(zl),
                    jnp.where((lane >= o2) & (lane < o3), jax.nn.sigmoid(zl), zl))
    lora = _mm(act.astype(BF16), wl_ref[...])
    r = zr_ref[:, 0:d_rwkv]
    k = zr_ref[:, d_rwkv:2 * d_rwkv]
    v = zr_ref[:, 2 * d_rwkv:3 * d_rwkv]
    lw = -math.exp(-0.5) * jax.nn.sigmoid(w0 + lora[:, 0:d_rwkv])
    a = jax.nn.sigmoid(a0 + lora[:, d_rwkv:2 * d_rwkv])
    gate = lora[:, 2 * d_rwkv:3 * d_rwkv]
    if first_layer:
        vfirst_out_ref[...] = v
    else:
        vgate = jax.nn.sigmoid(mv0 + lora[:, 3 * d_rwkv:4 * d_rwkv])
        v = v + (vfirst_ref[...] - v) * vgate
    kk = k * k_k
    kk = kk / jnp.maximum(jnp.sqrt(head_sum(kk * kk)), 1e-12)
    k_mod = k * (1.0 + (a - 1.0) * k_a)
    r_s[...] = r
    lw_s[...] = lw
    k_s[...] = k_mod
    v_s[...] = v
    kk_s[...] = kk
    b_s[...] = kk * a

    consts = _chunk_consts()

    def chunk_body(ci, carry):
        rows = pl.ds(pl.multiple_of(ci * c, c), c)
        for p in range(n_pairs):
            ls = slice(p * LANES, (p + 1) * LANES)
            y, st = _rwkv_chunk_pair(r_s[rows, ls], lw_s[rows, ls], k_s[rows, ls], v_s[rows, ls],
                                     kk_s[rows, ls], b_s[rows, ls], state_s[p], consts)
            y_s[rows, ls] = y
            state_s[p] = st
        return carry

    lax.fori_loop(0, tc // c, chunk_body, 0)

    y = y_s[...]
    inv_n = 1.0 / HEAD_DIM
    mean = head_sum(y) * inv_n
    dlt = y - mean
    var = head_sum(dlt * dlt) * inv_n
    yn = dlt * lax.rsqrt(var + GN_EPS) * gn_w + gn_b
    bonus = head_sum(r * k_mod * r_k) * v
    out_ref[...] = ((yn + bonus) * gate).astype(out_ref.dtype)


def _rwkv(zr, zl, vfirst, wl, vecs, hs, *, batch, seq, tc=256):
    t = zr.shape[0]
    d_rwkv = zr.shape[1] // 3
    first_layer = vfirst is None
    tiles = seq // tc
    row = lambda b, i: (b * tiles + i, 0)
    const = lambda b, i: (0, 0)
    in_specs = [pl.BlockSpec((tc, 3 * d_rwkv), row), pl.BlockSpec((tc, LORA_PAD), row)]
    args = [zr, zl]
    if not first_layer:
        in_specs.append(pl.BlockSpec((tc, d_rwkv), row))
        args.append(vfirst)
    in_specs += [pl.BlockSpec(wl.shape, const), pl.BlockSpec(vecs.shape, const),
                 pl.BlockSpec(hs.shape, const)]
    args += [wl, vecs, hs]
    out_specs = [pl.BlockSpec((tc, d_rwkv), row)]
    out_shape = [jax.ShapeDtypeStruct((t, d_rwkv), BF16)]
    if first_layer:
        out_specs.append(pl.BlockSpec((tc, d_rwkv), row))
        out_shape.append(jax.ShapeDtypeStruct((t, d_rwkv), F32))
    kern = functools.partial(_rwkv_kernel, first_layer=first_layer, tiles_per_seq=tiles,
                             d_rwkv=d_rwkv)
    res = pl.pallas_call(
        kern,
        grid=(batch, tiles),
        in_specs=in_specs,
        out_specs=out_specs,
        out_shape=out_shape,
        scratch_shapes=[pltpu.VMEM((tc, d_rwkv), F32) for _ in range(7)]
        + [pltpu.VMEM((d_rwkv // LANES, LANES, LANES), F32)],
        compiler_params=pltpu.CompilerParams(
            dimension_semantics=("arbitrary", "arbitrary"), vmem_limit_bytes=VMEM_LIMIT),
        name="rwkv7_first" if first_layer else "rwkv7",
    )(*args)
    return (res[0], res[1]) if first_layer else (res[0], vfirst)


def _outproj_kernel(o1_ref, o2_ref, o3_ref, l1_ref, l2_ref, l3_ref, rw_ref, x_ref,
                    ga_ref, gp_ref, w_ref, out_ref):
    l1, l2, l3 = l1_ref[...], l2_ref[...], l3_ref[...]
    top = jnp.maximum(jnp.maximum(l1, l2), l3)
    w1, w2, w3 = jnp.exp(l1 - top), jnp.exp(l2 - top), jnp.exp(l3 - top)
    num = w1 * o1_ref[...].astype(F32) + w2 * o2_ref[...].astype(F32) + w3 * o3_ref[...].astype(F32)
    attn = num / (w1 + w2 + w3)
    attn = _rms(attn, ga_ref[...]).astype(BF16)
    d_attn = attn.shape[1]
    mixed = _mm(attn, w_ref[0:d_attn, :]) + _mm(rw_ref[...], w_ref[d_attn:, :])
    out_ref[...] = x_ref[...] + _rms(mixed, gp_ref[...])


def _outproj(os_, lses, rw, x2, ga, gp, w, *, tm=512):
    t, d = x2.shape
    d_attn = os_[0].shape[1]
    d_rwkv = rw.shape[1]
    row = lambda i: (i, 0)
    const = lambda i: (0, 0)
    return pl.pallas_call(
        _outproj_kernel,
        grid=(t // tm,),
        in_specs=[pl.BlockSpec((tm, d_attn), row)] * 6
        + [pl.BlockSpec((tm, d_rwkv), row), pl.BlockSpec((tm, d), row),
           pl.BlockSpec((1, d_attn), const), pl.BlockSpec((1, d), const),
           pl.BlockSpec(w.shape, const)],
        out_specs=pl.BlockSpec((tm, d), row),
        out_shape=jax.ShapeDtypeStruct((t, d), F32),
        compiler_params=pltpu.CompilerParams(
            dimension_semantics=("arbitrary",), vmem_limit_bytes=VMEM_LIMIT),
        name="outproj",
    )(*os_, *lses, rw, x2, ga, gp, w)


def _ffn_kernel(x_ref, g1_ref, g2_ref, wu_ref, wd_ref, out_ref, acc_ref, *, ff_chunk):
    x = x_ref[...]
    h = _rms(x, g1_ref[...]).astype(BF16)
    d_ff = wu_ref.shape[1]
    for c in range(d_ff // ff_chunk):
        cs = slice(c * ff_chunk, (c + 1) * ff_chunk)
        u = jnp.maximum(_mm(h, wu_ref[:, cs]), 0.0)
        part = _mm((u * u).astype(BF16), wd_ref[cs, :])
        if c == 0:
            acc_ref[...] = part
        else:
            acc_ref[...] += part
    out_ref[...] = x + _rms(acc_ref[...], g2_ref[...])


def _ffn(x2, g1, g2, wu, wd, *, tm=512, ff_chunk=1024):
    t, d = x2.shape
    row = lambda i: (i, 0)
    const = lambda i: (0, 0)
    return pl.pallas_call(
        functools.partial(_ffn_kernel, ff_chunk=ff_chunk),
        grid=(t // tm,),
        in_specs=[pl.BlockSpec((tm, d), row), pl.BlockSpec((1, d), const),
                  pl.BlockSpec((1, d), const), pl.BlockSpec(wu.shape, const),
                  pl.BlockSpec(wd.shape, const)],
        out_specs=pl.BlockSpec((tm, d), row),
        out_shape=jax.ShapeDtypeStruct((t, d), F32),
        scratch_shapes=[pltpu.VMEM((tm, d), F32)],
        compiler_params=pltpu.CompilerParams(
            dimension_semantics=("arbitrary",), vmem_limit_bytes=VMEM_LIMIT),
        name="ffn",
    )(x2, g1, g2, wu, wd)


def _layer_params(i, d_attn, d_rwkv, w_in_first, w_in_rest, mu_shift, mu_shift_mv,
                  decay_up, aaa_up, gate_up, mv_up):
    w = w_in_first if i == 0 else w_in_rest[i - 1]
    d = w.shape[0]
    n_first = w_in_first.shape[1]
    lora_used = LORA_DECAY + LORA_AAA + LORA_GATE + LORA_MV
    pieces = [w[:, :n_first]]
    mus = [mu_shift[i]]
    if i == 0:
        pieces.append(jnp.zeros((d, LORA_MV), F32))
        mus.append(jnp.zeros((LORA_MV,), F32))
    else:
        pieces.append(w[:, n_first:])
        mus.append(mu_shift_mv[i - 1])
    pieces.append(jnp.zeros((d, LORA_PAD - lora_used), F32))
    mus.append(jnp.zeros((LORA_PAD - lora_used,), F32))
    w_r = jnp.concatenate(pieces, axis=1).astype(BF16)
    mu_r = jnp.concatenate(mus)[None, :]
    wl = jnp.zeros((LORA_PAD, 4 * d_rwkv), F32)
    o1 = LORA_DECAY
    o2 = o1 + LORA_AAA
    o3 = o2 + LORA_GATE
    wl = wl.at[0:o1, 0:d_rwkv].set(decay_up[i])
    wl = wl.at[o1:o2, d_rwkv:2 * d_rwkv].set(aaa_up[i])
    wl = wl.at[o2:o3, 2 * d_rwkv:3 * d_rwkv].set(gate_up[i])
    if i > 0:
        wl = wl.at[o3:o3 + LORA_MV, 3 * d_rwkv:4 * d_rwkv].set(mv_up[i - 1])
    return w_r, mu_r, wl.astype(BF16)


def kernel(x, norm_mix_pre, norm_mix_post, norm_ffn_pre, norm_ffn_post, w_in_first, w_in_rest,
           mu_shift, mu_shift_mv, attn_out_gain, decay_w0, decay_up, aaa_a0, aaa_up, mv_v0, mv_up,
           gate_up, k_k, k_a, r_k, gn_w, gn_b, w_out, w_ffn_up, w_ffn_down):
    batch, seq, d = x.shape
    depth = norm_mix_pre.shape[0]
    d_attn = attn_out_gain.shape[1]
    d_rwkv = decay_w0.shape[1]
    head = lax.broadcasted_iota(jnp.int32, (d_rwkv, d_rwkv), 0) // HEAD_DIM
    hs = (head == head.T).astype(BF16)
    x2 = x.reshape(batch * seq, d)
    vfirst = None
    for i in range(depth):
        w_r, mu_r, wl = _layer_params(i, d_attn, d_rwkv, w_in_first, w_in_rest, mu_shift,
                                      mu_shift_mv, decay_up, aaa_up, gate_up, mv_up)
        mv0 = mv_v0[i - 1] if i > 0 else jnp.zeros((d_rwkv,), F32)
        vecs = jnp.stack([decay_w0[i], aaa_a0[i], mv0, k_k[i], k_a[i], r_k[i].reshape(-1),
                          gn_w[i], gn_b[i]])
        za, zr, zl = _inproj(x2, norm_mix_pre[i][None, :], w_r, mu_r, seq=seq)
        branches = [_attn_branch(za, batch=batch, seq=seq, d_attn=d_attn, dil=dil)
                    for dil in DILATIONS]
        rw, vfirst = _rwkv(zr, zl, vfirst, wl, vecs, hs, batch=batch, seq=seq)
        x2 = _outproj([o for o, _ in branches], [l for _, l in branches], rw, x2,
                      attn_out_gain[i][None, :], norm_mix_post[i][None, :], w_out[i].astype(BF16))
        x2 = _ffn(x2, norm_ffn_pre[i][None, :], norm_ffn_post[i][None, :],
                  w_ffn_up[i].astype(BF16), w_ffn_down[i].astype(BF16))
    return x2.reshape(batch, seq, d)
```

```python
import functools
import math

import jax
import jax.numpy as jnp
from jax import lax
from jax.experimental import pallas as pl
from jax.experimental.pallas import tpu as pltpu

F32 = jnp.float32
BF16 = jnp.bfloat16

HEAD_DIM = 64
LANES = 128
NORM_EPS = 1e-6
GN_EPS = 64e-5
DILATIONS = (1, 4, 16)
BAND = 128
LORA_DECAY, LORA_AAA, LORA_GATE, LORA_MV = 32, 32, 96, 32
LORA_PAD = 256
CHUNK = 64
NEG_BIG = -1e30
VMEM_LIMIT = 56 * 1024 * 1024


def _nt(a, b):
    return lax.dot_general(a, b, (((1,), (1,)), ((), ())), preferred_element_type=F32)


def _tn(a, b):
    return lax.dot_general(a, b, (((0,), (0,)), ((), ())), preferred_element_type=F32)


def _mm(a, b):
    return jnp.dot(a, b, preferred_element_type=F32)


def _split2(x):
    hi = x.astype(BF16)
    lo = (x - hi.astype(F32)).astype(BF16)
    return hi, lo


def _rms(x, g):
    return x * lax.rsqrt(jnp.mean(x * x, axis=-1, keepdims=True) + NORM_EPS) * g


def _inproj_kernel(x_ref, g_ref, w_ref, mu_ref, za_ref, zr_ref, zl_ref, carry_ref,
                   *, seq_tiles, d_attn3, d_rwkv3):
    i = pl.program_id(0)
    tm = x_ref.shape[0]
    h = _rms(x_ref[...], g_ref[...]).astype(BF16)
    cw = 512
    for c in range(d_attn3 // cw):
        z = _mm(h, w_ref[:, c * cw:(c + 1) * cw])
        za_ref[:, c * cw:(c + 1) * cw] = z.astype(BF16)
    first_tile = (i % seq_tiles) == 0
    row0 = lax.broadcasted_iota(jnp.int32, (tm, 1), 0) == 0
    n_shift = d_rwkv3 + LORA_PAD
    c0 = 0
    while c0 < n_shift:
        w = min(cw, n_shift - c0)
        z = _mm(h, w_ref[:, d_attn3 + c0:d_attn3 + c0 + w])
        carry = jnp.where(first_tile, 0.0, carry_ref[:, c0:c0 + w])
        prev = jnp.where(row0, carry, pltpu.roll(z, 1, 0))
        carry_ref[:, c0:c0 + w] = z[tm - 1:tm, :]
        zs = z + (prev - z) * mu_ref[:, c0:c0 + w]
        if c0 < d_rwkv3:
            zr_ref[:, c0:c0 + w] = zs
        else:
            zl_ref[...] = zs
        c0 += w


def _inproj(x2, g, w, mu, *, seq, tm=512):
    t, d = x2.shape
    nc = w.shape[1]
    d_attn3 = 3 * (d // 2)
    d_rwkv3 = nc - d_attn3 - LORA_PAD
    kern = functools.partial(_inproj_kernel, seq_tiles=seq // tm, d_attn3=d_attn3, d_rwkv3=d_rwkv3)
    return pl.pallas_call(
        kern,
        grid=(t // tm,),
        in_specs=[
            pl.BlockSpec((tm, d), lambda i: (i, 0)),
            pl.BlockSpec((1, d), lambda i: (0, 0)),
            pl.BlockSpec((d, nc), lambda i: (0, 0)),
            pl.BlockSpec((1, nc - d_attn3), lambda i: (0, 0)),
        ],
        out_specs=[
            pl.BlockSpec((tm, d_attn3), lambda i: (i, 0)),
            pl.BlockSpec((tm, d_rwkv3), lambda i: (i, 0)),
            pl.BlockSpec((tm, LORA_PAD), lambda i: (i, 0)),
        ],
        out_shape=[
            jax.ShapeDtypeStruct((t, d_attn3), BF16),
            jax.ShapeDtypeStruct((t, d_rwkv3), F32),
            jax.ShapeDtypeStruct((t, LORA_PAD), F32),
        ],
        scratch_shapes=[pltpu.VMEM((1, nc - d_attn3), F32)],
        compiler_params=pltpu.CompilerParams(
            dimension_semantics=("arbitrary",), vmem_limit_bytes=VMEM_LIMIT),
        name="inproj",
    )(x2, g, w, mu)


def _attn_kernel(q_ref, k_ref, v_ref, o_ref, lse_ref):
    sub_len = q_ref.shape[0]
    lane = lax.broadcasted_iota(jnp.int32, (1, LANES), 1)
    head0 = lane < HEAD_DIM
    qrow = lax.broadcasted_iota(jnp.int32, (BAND, 1), 0)
    kcol = lax.broadcasted_iota(jnp.int32, (1, 2 * BAND), 1)
    scale = 1.0 / math.sqrt(HEAD_DIM)

    def body(i, carry):
        qs = pl.multiple_of(i * BAND, BAND)
        ws = pl.multiple_of(jnp.maximum(i * BAND - BAND, 0), BAND)
        q = q_ref[pl.ds(qs, BAND), :]
        kw = k_ref[pl.ds(ws, 2 * BAND), :]
        vw = v_ref[pl.ds(ws, 2 * BAND), :]
        rel = (qs - ws) + qrow - kcol
        valid = (rel >= 0) & (rel <= BAND)
        outs, lses = [], []
        for hsel in (head0, jnp.logical_not(head0)):
            qh = jnp.where(hsel, q, jnp.zeros_like(q))
            s = _nt(qh, kw) * scale
            s = jnp.where(valid, s, NEG_BIG)
            m = jnp.max(s, axis=-1, keepdims=True)
            p = jnp.exp(s - m)
            l = jnp.sum(p, axis=-1, keepdims=True)
            pv = _mm(p.astype(BF16), vw)
            outs.append(pv / l)
            lses.append(m + jnp.log(l))
        o_ref[pl.ds(qs, BAND), :] = jnp.where(head0, outs[0], outs[1]).astype(o_ref.dtype)
        lse_ref[pl.ds(qs, BAND), :] = jnp.where(head0, lses[0], lses[1])
        return carry

    lax.fori_loop(0, sub_len // BAND, body, 0)


def _attn_branch(za, *, batch, seq, d_attn, dil):
    sub_len = seq // dil
    pairs = d_attn // LANES
    cols = 3 * pairs
    zv = za.reshape(batch, sub_len, dil * 3 * d_attn)
    blk = (None, sub_len, LANES)
    o, lse = pl.pallas_call(
        _attn_kernel,
        grid=(batch, dil, pairs),
        in_specs=[
            pl.BlockSpec(blk, lambda b, r, p: (b, 0, r * cols + p)),
            pl.BlockSpec(blk, lambda b, r, p: (b, 0, r * cols + pairs + p)),
            pl.BlockSpec(blk, lambda b, r, p: (b, 0, r * cols + 2 * pairs + p)),
        ],
        out_specs=[
            pl.BlockSpec(blk, lambda b, r, p: (b, 0, r * pairs + p)),
            pl.BlockSpec(blk, lambda b, r, p: (b, 0, r * pairs + p)),
        ],
        out_shape=[
            jax.ShapeDtypeStruct((batch, sub_len, dil * d_attn), BF16),
            jax.ShapeDtypeStruct((batch, sub_len, dil * d_attn), F32),
        ],
        compiler_params=pltpu.CompilerParams(
            dimension_semantics=("arbitrary", "arbitrary", "arbitrary"),
            vmem_limit_bytes=VMEM_LIMIT),
        name=f"attn_d{dil}",
    )(zv, zv, zv)
    return o.reshape(batch * seq, d_attn), lse.reshape(batch * seq, d_attn)


def _block_diag(x, head0):
    zero = jnp.zeros_like(x)
    return jnp.concatenate([jnp.where(head0, x, zero), jnp.where(head0, zero, x)], axis=0)


def _chunk_consts():
    c = CHUNK
    trow = lax.broadcasted_iota(jnp.int32, (c, 1), 0)
    col3 = lax.broadcasted_iota(jnp.int32, (1, 3 * c), 1)
    tri3 = jnp.where((col3 & (c - 1)) <= trow, 1.0, 0.0).astype(BF16)
    lane_w = lax.broadcasted_iota(jnp.int32, (1, LANES), 1)
    scol = lane_w & (HEAD_DIM - 1)
    strict = scol < trow
    incl = scol <= trow
    eye_w = jnp.where(scol == trow, 1.0, 0.0).astype(F32)
    head0 = lane_w < HEAD_DIM
    rr = lax.broadcasted_iota(jnp.int32, (LANES, 1), 0)
    same_head = (rr // HEAD_DIM) == (lane_w // HEAD_DIM)
    levels = []
    s = 1
    while s < c:
        in_pair = (trow // (2 * s)) == (scol // (2 * s))
        levels.append(in_pair & ((trow // s) != (scol // s)) & strict)
        s *= 2
    return dict(tri3=tri3, strict=strict, incl=incl, eye_w=eye_w, head0=head0,
                same_head=same_head, levels=levels)


def _chunk_precompute(ins, consts):
    c = CHUNK
    head0 = consts["head0"]
    strict, incl, same_head = consts["strict"], consts["incl"], consts["same_head"]
    levels = consts["levels"]
    zero = jnp.zeros((c, LANES), F32)
    zero_sq = jnp.zeros((LANES, LANES), F32)

    def bd(x):
        return _block_diag(x, head0).astype(BF16)

    cums = []
    for (_, lw, _, _, _, _) in ins:
        hi = lw.astype(BF16)
        r1 = lw - hi.astype(F32)
        mid = r1.astype(BF16)
        lo = (r1 - mid.astype(F32)).astype(BF16)
        cums.append(_mm(consts["tri3"], jnp.concatenate([hi, mid, lo], axis=0)))
    ops = []
    for (r, lw, k, v, kk, b), cum in zip(ins, cums):
        cum_end = cum[c - 1:c, :]
        e_neg = jnp.exp(-cum)
        e_end = jnp.exp(cum_end - cum)
        ops.append(dict(kq=kk * jnp.exp(cum - lw), rq=r * jnp.exp(cum), kt=k * e_neg, bt=b * e_neg,
                        kh=(k * e_end).astype(BF16), bh=(b * e_end).astype(BF16),
                        g=jnp.exp(cum_end), v=v))
    a_all = [_nt(jnp.concatenate([o["kq"], o["rq"]], axis=0).astype(BF16),
                 jnp.concatenate([bd(o["bt"]), bd(o["kt"])], axis=0)) for o in ops]
    a_ab = [jnp.where(strict, a[:c, :LANES], zero) for a in a_all]
    a_rb = [jnp.where(incl, a[c:, :LANES], zero).astype(BF16) for a in a_all]
    av = [_mm(jnp.concatenate([jnp.where(strict, a[:c, LANES:], zero),
                               jnp.where(incl, a[c:, LANES:], zero)], axis=0).astype(BF16),
              bd(o["v"])) for a, o in zip(a_all, ops)]
    t_inv = [consts["eye_w"] - jnp.where(levels[0], a, zero) for a in a_ab]
    for mask in levels[1:]:
        x = [_mm(jnp.where(mask, a, zero).astype(BF16), bd(t)) for a, t in zip(a_ab, t_inv)]
        t_inv = [t - _mm(t.astype(BF16), bd(xi)) for t, xi in zip(t_inv, x)]
    twu = [_mm(t.astype(BF16), jnp.concatenate([bd(o["kq"]), bd(a[:c])], axis=1))
           for t, o, a in zip(t_inv, ops, av)]
    arb = [_mm(ar, jnp.concatenate([bd(x[:, :LANES]), bd(x[:, LANES:])], axis=1))
           for ar, x in zip(a_rb, twu)]
    out = []
    for o, a, x, y in zip(ops, av, twu, arb):
        w = x[:, :LANES].astype(BF16)
        u0 = x[:, LANES:]
        pm = -jnp.where(same_head, _tn(w, o["bh"]), zero_sq)
        dd = jnp.where(same_head,
                       _tn(jnp.concatenate([o["v"], -u0], axis=0).astype(BF16),
                           jnp.concatenate([o["kh"], o["bh"]], axis=0)), zero_sq)
        out.append((o["rq"] - y[:, :LANES], a[c:] - y[:, LANES:], pm, dd, o["g"]))
    return out


def _chunk_apply(pre, states):
    sb = [s.astype(BF16) for s in states]
    ys = [_nt(p[0].astype(BF16), s) + p[1] for p, s in zip(pre, sb)]
    new = [st * p[4] + _mm(s, p[2].astype(BF16)) + p[3] for p, s, st in zip(pre, sb, states)]
    return ys, new


def _rwkv_kernel(*refs, first_layer, tiles_per_seq, d_rwkv):
    if first_layer:
        (zr_ref, zl_ref, wl_ref, vec_ref, hs_ref, out_ref, vfirst_out_ref,
         r_s, lw_s, k_s, v_s, kk_s, b_s, y_s, state_s) = refs
        vfirst_ref = None
    else:
        (zr_ref, zl_ref, vfirst_ref, wl_ref, vec_ref, hs_ref, out_ref,
         r_s, lw_s, k_s, v_s, kk_s, b_s, y_s, state_s) = refs
    tc = zr_ref.shape[0]
    n_pairs = d_rwkv // LANES
    c = CHUNK

    @pl.when(pl.program_id(1) == 0)
    def _():
        state_s[...] = jnp.zeros_like(state_s)

    w0, a0, mv0, k_k, k_a, r_k, gn_w, gn_b = (vec_ref[j:j + 1, :] for j in range(8))
    hs = hs_ref[...]

    def head_sum(x):
        hi, lo = _split2(x)
        return _mm(hi, hs) + _mm(lo, hs)

    zl = zl_ref[...]
    lane = lax.broadcasted_iota(jnp.int32, (1, LORA_PAD), 1)
    o1 = LORA_DECAY
    o2 = o1 + LORA_AAA
    o3 = o2 + LORA_GATE
    act = jnp.where(lane < o1, jnp.tanh(zl),
                    jnp.where((lane >= o2) & (lane < o3), jax.nn.sigmoid(zl), zl))
    lora = _mm(act.astype(BF16), wl_ref[...])
    r = zr_ref[:, 0:d_rwkv]
    k = zr_ref[:, d_rwkv:2 * d_rwkv]
    v = zr_ref[:, 2 * d_rwkv:3 * d_rwkv]
    lw = -math.exp(-0.5) * jax.nn.sigmoid(w0 + lora[:, 0:d_rwkv])
    a = jax.nn.sigmoid(a0 + lora[:, d_rwkv:2 * d_rwkv])
    gate = lora[:, 2 * d_rwkv:3 * d_rwkv]
    if first_layer:
        vfirst_out_ref[...] = v
    else:
        vgate = jax.nn.sigmoid(mv0 + lora[:, 3 * d_rwkv:4 * d_rwkv])
        v = v + (vfirst_ref[...] - v) * vgate
    kk = k * k_k
    kk = kk / jnp.maximum(jnp.sqrt(head_sum(kk * kk)), 1e-12)
    k_mod = k * (1.0 + (a - 1.0) * k_a)
    r_s[...] = r
    lw_s[...] = lw
    k_s[...] = k_mod
    v_s[...] = v
    kk_s[...] = kk
    b_s[...] = kk * a

    consts = _chunk_consts()

    n_chunks = tc // c
    ins = []
    for ci in range(n_chunks):
        rows = slice(ci * c, (ci + 1) * c)
        for p in range(n_pairs):
            ls = slice(p * LANES, (p + 1) * LANES)
            ins.append(tuple(ref[rows, ls] for ref in (r_s, lw_s, k_s, v_s, kk_s, b_s)))
    pre = _chunk_precompute(ins, consts)
    states = [state_s[p] for p in range(n_pairs)]
    for ci in range(n_chunks):
        ys, states = _chunk_apply(pre[ci * n_pairs:(ci + 1) * n_pairs], states)
        for p in range(n_pairs):
            y_s[ci * c:(ci + 1) * c, p * LANES:(p + 1) * LANES] = ys[p]
    for p in range(n_pairs):
        state_s[p] = states[p]

    y = y_s[...]
    inv_n = 1.0 / HEAD_DIM
    mean = head_sum(y) * inv_n
    dlt = y - mean
    var = head_sum(dlt * dlt) * inv_n
    yn = dlt * lax.rsqrt(var + GN_EPS) * gn_w + gn_b
    bonus = head_sum(r * k_mod * r_k) * v
    out_ref[...] = ((yn + bonus) * gate).astype(out_ref.dtype)


def _rwkv(zr, zl, vfirst, wl, vecs, hs, *, batch, seq, tc=256):
    t = zr.shape[0]
    d_rwkv = zr.shape[1] // 3
    first_layer = vfirst is None
    tiles = seq // tc
    row = lambda b, i: (b * tiles + i, 0)
    const = lambda b, i: (0, 0)
    in_specs = [pl.BlockSpec((tc, 3 * d_rwkv), row), pl.BlockSpec((tc, LORA_PAD), row)]
    args = [zr, zl]
    if not first_layer:
        in_specs.append(pl.BlockSpec((tc, d_rwkv), row))
        args.append(vfirst)
    in_specs += [pl.BlockSpec(wl.shape, const), pl.BlockSpec(vecs.shape, const),
                 pl.BlockSpec(hs.shape, const)]
    args += [wl, vecs, hs]
    out_specs = [pl.BlockSpec((tc, d_rwkv), row)]
    out_shape = [jax.ShapeDtypeStruct((t, d_rwkv), BF16)]
    if first_layer:
        out_specs.append(pl.BlockSpec((tc, d_rwkv), row))
        out_shape.append(jax.ShapeDtypeStruct((t, d_rwkv), F32))
    kern = functools.partial(_rwkv_kernel, first_layer=first_layer, tiles_per_seq=tiles,
                             d_rwkv=d_rwkv)
    res = pl.pallas_call(
        kern,
        grid=(batch, tiles),
        in_specs=in_specs,
        out_specs=out_specs,
        out_shape=out_shape,
        scratch_shapes=[pltpu.VMEM((tc, d_rwkv), F32) for _ in range(7)]
        + [pltpu.VMEM((d_rwkv // LANES, LANES, LANES), F32)],
        compiler_params=pltpu.CompilerParams(
            dimension_semantics=("arbitrary", "arbitrary"), vmem_limit_bytes=VMEM_LIMIT),
        name="rwkv7_first" if first_layer else "rwkv7",
    )(*args)
    return (res[0], res[1]) if first_layer else (res[0], vfirst)


def _outproj_kernel(o1_ref, o2_ref, o3_ref, l1_ref, l2_ref, l3_ref, rw_ref, x_ref,
                    ga_ref, gp_ref, w_ref, out_ref):
    l1, l2, l3 = l1_ref[...], l2_ref[...], l3_ref[...]
    top = jnp.maximum(jnp.maximum(l1, l2), l3)
    w1, w2, w3 = jnp.exp(l1 - top), jnp.exp(l2 - top), jnp.exp(l3 - top)
    num = w1 * o1_ref[...].astype(F32) + w2 * o2_ref[...].astype(F32) + w3 * o3_ref[...].astype(F32)
    attn = num / (w1 + w2 + w3)
    attn = _rms(attn, ga_ref[...]).astype(BF16)
    d_attn = attn.shape[1]
    mixed = _mm(attn, w_ref[0:d_attn, :]) + _mm(rw_ref[...], w_ref[d_attn:, :])
    out_ref[...] = x_ref[...] + _rms(mixed, gp_ref[...])


def _outproj(os_, lses, rw, x2, ga, gp, w, *, tm=512):
    t, d = x2.shape
    d_attn = os_[0].shape[1]
    d_rwkv = rw.shape[1]
    row = lambda i: (i, 0)
    const = lambda i: (0, 0)
    return pl.pallas_call(
        _outproj_kernel,
        grid=(t // tm,),
        in_specs=[pl.BlockSpec((tm, d_attn), row)] * 6
        + [pl.BlockSpec((tm, d_rwkv), row), pl.BlockSpec((tm, d), row),
           pl.BlockSpec((1, d_attn), const), pl.BlockSpec((1, d), const),
           pl.BlockSpec(w.shape, const)],
        out_specs=pl.BlockSpec((tm, d), row),
        out_shape=jax.ShapeDtypeStruct((t, d), F32),
        compiler_params=pltpu.CompilerParams(
            dimension_semantics=("arbitrary",), vmem_limit_bytes=VMEM_LIMIT),
        name="outproj",
    )(*os_, *lses, rw, x2, ga, gp, w)


def _ffn_kernel(x_ref, g1_ref, g2_ref, wu_ref, wd_ref, out_ref, acc_ref, *, ff_chunk):
    x = x_ref[...]
    h = _rms(x, g1_ref[...]).astype(BF16)
    d_ff = wu_ref.shape[1]
    for c in range(d_ff // ff_chunk):
        cs = slice(c * ff_chunk, (c + 1) * ff_chunk)
        u = jnp.maximum(_mm(h, wu_ref[:, cs]), 0.0)
        part = _mm((u * u).astype(BF16), wd_ref[cs, :])
        if c == 0:
            acc_ref[...] = part
        else:
            acc_ref[...] += part
    out_ref[...] = x + _rms(acc_ref[...], g2_ref[...])


def _ffn(x2, g1, g2, wu, wd, *, tm=512, ff_chunk=1024):
    t, d = x2.shape
    row = lambda i: (i, 0)
    const = lambda i: (0, 0)
    return pl.pallas_call(
        functools.partial(_ffn_kernel, ff_chunk=ff_chunk),
        grid=(t // tm,),
        in_specs=[pl.BlockSpec((tm, d), row), pl.BlockSpec((1, d), const),
                  pl.BlockSpec((1, d), const), pl.BlockSpec(wu.shape, const),
                  pl.BlockSpec(wd.shape, const)],
        out_specs=pl.BlockSpec((tm, d), row),
        out_shape=jax.ShapeDtypeStruct((t, d), F32),
        scratch_shapes=[pltpu.VMEM((tm, d), F32)],
        compiler_params=pltpu.CompilerParams(
            dimension_semantics=("arbitrary",), vmem_limit_bytes=VMEM_LIMIT),
        name="ffn",
    )(x2, g1, g2, wu, wd)


def _layer_params(i, d_attn, d_rwkv, w_in_first, w_in_rest, mu_shift, mu_shift_mv,
                  decay_up, aaa_up, gate_up, mv_up):
    w = w_in_first if i == 0 else w_in_rest[i - 1]
    d = w.shape[0]
    n_first = w_in_first.shape[1]
    lora_used = LORA_DECAY + LORA_AAA + LORA_GATE + LORA_MV
    pieces = [w[:, :n_first]]
    mus = [mu_shift[i]]
    if i == 0:
        pieces.append(jnp.zeros((d, LORA_MV), F32))
        mus.append(jnp.zeros((LORA_MV,), F32))
    else:
        pieces.append(w[:, n_first:])
        mus.append(mu_shift_mv[i - 1])
    pieces.append(jnp.zeros((d, LORA_PAD - lora_used), F32))
    mus.append(jnp.zeros((LORA_PAD - lora_used,), F32))
    w_r = jnp.concatenate(pieces, axis=1).astype(BF16)
    mu_r = jnp.concatenate(mus)[None, :]
    wl = jnp.zeros((LORA_PAD, 4 * d_rwkv), F32)
    o1 = LORA_DECAY
    o2 = o1 + LORA_AAA
    o3 = o2 + LORA_GATE
    wl = wl.at[0:o1, 0:d_rwkv].set(decay_up[i])
    wl = wl.at[o1:o2, d_rwkv:2 * d_rwkv].set(aaa_up[i])
    wl = wl.at[o2:o3, 2 * d_rwkv:3 * d_rwkv].set(gate_up[i])
    if i > 0:
        wl = wl.at[o3:o3 + LORA_MV, 3 * d_rwkv:4 * d_rwkv].set(mv_up[i - 1])
    return w_r, mu_r, wl.astype(BF16)


def kernel(x, norm_mix_pre, norm_mix_post, norm_ffn_pre, norm_ffn_post, w_in_first, w_in_rest,
           mu_shift, mu_shift_mv, attn_out_gain, decay_w0, decay_up, aaa_a0, aaa_up, mv_v0, mv_up,
           gate_up, k_k, k_a, r_k, gn_w, gn_b, w_out, w_ffn_up, w_ffn_down):
    batch, seq, d = x.shape
    depth = norm_mix_pre.shape[0]
    d_attn = attn_out_gain.shape[1]
    d_rwkv = decay_w0.shape[1]
    head = lax.broadcasted_iota(jnp.int32, (d_rwkv, d_rwkv), 0) // HEAD_DIM
    hs = (head == head.T).astype(BF16)
    x2 = x.reshape(batch * seq, d)
    vfirst = None
    for i in range(depth):
        w_r, mu_r, wl = _layer_params(i, d_attn, d_rwkv, w_in_first, w_in_rest, mu_shift,
                                      mu_shift_mv, decay_up, aaa_up, gate_up, mv_up)
        mv0 = mv_v0[i - 1] if i > 0 else jnp.zeros((d_rwkv,), F32)
        vecs = jnp.stack([decay_w0[i], aaa_a0[i], mv0, k_k[i], k_a[i], r_k[i].reshape(-1),
                          gn_w[i], gn_b[i]])
        za, zr, zl = _inproj(x2, norm_mix_pre[i][None, :], w_r, mu_r, seq=seq)
        branches = [_attn_branch(za, batch=batch, seq=seq, d_attn=d_attn, dil=dil)
                    for dil in DILATIONS]
        rw, vfirst = _rwkv(zr, zl, vfirst, wl, vecs, hs, batch=batch, seq=seq)
        x2 = _outproj([o for o, _ in branches], [l for _, l in branches], rw, x2,
                      attn_out_gain[i][None, :], norm_mix_post[i][None, :], w_out[i].astype(BF16))
        x2 = _ffn(x2, norm_ffn_pre[i][None, :], norm_ffn_post[i][None, :],
                  w_ffn_up[i].astype(BF16), w_ffn_down[i].astype(BF16))
    return x2.reshape(batch, seq, d)
```

```python
import functools
import math

import jax
import jax.numpy as jnp
from jax import lax
from jax.experimental import pallas as pl
from jax.experimental.pallas import tpu as pltpu

F32 = jnp.float32
BF16 = jnp.bfloat16

HEAD_DIM = 64
LANES = 128
NORM_EPS = 1e-6
GN_EPS = 64e-5
DILATIONS = (1, 4, 16)
BAND = 128
QKV_W = 3 * LANES
INPROJ_TM = BAND * DILATIONS[1]
ATTN_GROUP = 2
LORA_DECAY, LORA_AAA, LORA_GATE, LORA_MV = 32, 32, 96, 32
LORA_PAD = 256
CHUNK = 64
NEG_BIG = -1e30
VMEM_LIMIT = 56 * 1024 * 1024


def _nt(a, b):
    return lax.dot_general(a, b, (((1,), (1,)), ((), ())), preferred_element_type=F32)


def _tn(a, b):
    return lax.dot_general(a, b, (((0,), (0,)), ((), ())), preferred_element_type=F32)


def _mm(a, b):
    return jnp.dot(a, b, preferred_element_type=F32)


def _split2(x):
    hi = x.astype(BF16)
    lo = (x - hi.astype(F32)).astype(BF16)
    return hi, lo


def _rms(x, g):
    return x * lax.rsqrt(jnp.mean(x * x, axis=-1, keepdims=True) + NORM_EPS) * g


def _inproj_kernel(x_ref, g_ref, w_ref, mu_ref, x1_ref, x4_ref, x16_ref, zr_ref, zl_ref,
                   carry_ref, zs_ref, *, seq_tiles, n_pairs, d_rwkv3):
    i = pl.program_id(0)
    tm = x_ref.shape[0]
    h = _rms(x_ref[...], g_ref[...]).astype(BF16)
    for p in range(n_pairs):
        z = _mm(h, w_ref[:, p * QKV_W:(p + 1) * QKV_W])
        x1_ref[p, :, 0] = z.astype(BF16).reshape(tm // BAND, BAND, QKV_W)
        for j in range(QKV_W // LANES):
            ls = slice(j * LANES, (j + 1) * LANES)
            zs_ref[p, j] = z[:, ls]
            for d, ref in ((DILATIONS[1], x4_ref), (DILATIONS[2], x16_ref)):
                for r in range(d):
                    ref[p, r, :, ls] = zs_ref[p, j, pl.ds(r, tm // d, stride=d), :].astype(BF16)
    d_attn3 = n_pairs * QKV_W
    first_tile = (i % seq_tiles) == 0
    row0 = lax.broadcasted_iota(jnp.int32, (tm, 1), 0) == 0
    n_shift = d_rwkv3 + LORA_PAD
    cw = 512
    c0 = 0
    while c0 < n_shift:
        w = min(cw, n_shift - c0)
        z = _mm(h, w_ref[:, d_attn3 + c0:d_attn3 + c0 + w])
        carry = jnp.where(first_tile, 0.0, carry_ref[:, c0:c0 + w])
        prev = jnp.where(row0, carry, pltpu.roll(z, 1, 0))
        carry_ref[:, c0:c0 + w] = z[tm - 1:tm, :]
        zs = z + (prev - z) * mu_ref[:, c0:c0 + w]
        if c0 < d_rwkv3:
            zr_ref[:, c0:c0 + w] = zs
        else:
            zl_ref[...] = zs
        c0 += w


def _inproj(x2, g, w, mu, *, batch, seq):
    tm = INPROJ_TM
    t, d = x2.shape
    nc = w.shape[1]
    n_pairs = (d // 2) // LANES
    d_attn3 = n_pairs * QKV_W
    d_rwkv3 = nc - d_attn3 - LORA_PAD
    seq_tiles = seq // tm
    d4, d16 = DILATIONS[1], DILATIONS[2]
    assert tm == BAND * d4 and (BAND * d16) % tm == 0 and seq % (BAND * d16) == 0
    sub16 = BAND * d16 // tm
    kern = functools.partial(_inproj_kernel, seq_tiles=seq_tiles, n_pairs=n_pairs, d_rwkv3=d_rwkv3)
    return pl.pallas_call(
        kern,
        grid=(t // tm,),
        in_specs=[
            pl.BlockSpec((tm, d), lambda i: (i, 0)),
            pl.BlockSpec((1, d), lambda i: (0, 0)),
            pl.BlockSpec((d, nc), lambda i: (0, 0)),
            pl.BlockSpec((1, nc - d_attn3), lambda i: (0, 0)),
        ],
        out_specs=[
            pl.BlockSpec((n_pairs, None, tm // BAND, 1, BAND, QKV_W),
                         lambda i: (0, i // seq_tiles, i % seq_tiles, 0, 0, 0)),
            pl.BlockSpec((n_pairs, None, None, d4, BAND, QKV_W),
                         lambda i: (0, i // seq_tiles, i % seq_tiles, 0, 0, 0)),
            pl.BlockSpec((n_pairs, None, None, d16, tm // d16, QKV_W),
                         lambda i: (0, i // seq_tiles, (i % seq_tiles) // sub16, 0,
                                    (i % seq_tiles) % sub16, 0)),
            pl.BlockSpec((tm, d_rwkv3), lambda i: (i, 0)),
            pl.BlockSpec((tm, LORA_PAD), lambda i: (i, 0)),
        ],
        out_shape=[
            jax.ShapeDtypeStruct((n_pairs, batch, seq // BAND, 1, BAND, QKV_W), BF16),
            jax.ShapeDtypeStruct((n_pairs, batch, seq // (BAND * d4), d4, BAND, QKV_W), BF16),
            jax.ShapeDtypeStruct((n_pairs, batch, seq // (BAND * d16), d16, BAND, QKV_W), BF16),
            jax.ShapeDtypeStruct((t, d_rwkv3), F32),
            jax.ShapeDtypeStruct((t, LORA_PAD), F32),
        ],
        scratch_shapes=[pltpu.VMEM((1, nc - d_attn3), F32),
                        pltpu.VMEM((n_pairs, QKV_W // LANES, tm, LANES), F32)],
        compiler_params=pltpu.CompilerParams(
            dimension_semantics=("arbitrary",), vmem_limit_bytes=VMEM_LIMIT),
        name="inproj",
    )(x2, g, w, mu)


def _attn_blocks(x_ref, r, blocks, consts):
    head0, bias_first, bias_rest = consts
    scale = 1.0 / math.sqrt(HEAD_DIM)
    ops = []
    for n in blocks:
        prev = max(n - 1, 0) if isinstance(n, int) else jnp.maximum(n - 1, 0)
        q = x_ref[n, r, :, 0:LANES] * scale
        kw = jnp.concatenate([x_ref[prev, r, :, LANES:2 * LANES], x_ref[n, r, :, LANES:2 * LANES]],
                             axis=0)
        vw = jnp.concatenate([x_ref[prev, r, :, 2 * LANES:], x_ref[n, r, :, 2 * LANES:]], axis=0)
        bias = jnp.where(n == 0, bias_first, bias_rest)
        ops.append((q, kw, vw, bias))
    s = [_nt(_block_diag(q, head0), kw) for q, kw, _, _ in ops]
    s = [jnp.concatenate([si[:BAND] + o[3], si[BAND:] + o[3]], axis=0) for si, o in zip(s, ops)]
    m = [jnp.max(si, axis=-1, keepdims=True) for si in s]
    p = [jnp.exp(si - mi) for si, mi in zip(s, m)]
    l = [jnp.sum(pi, axis=-1, keepdims=True) for pi in p]
    pv = [_mm(pi.astype(BF16), o[2]) for pi, o in zip(p, ops)]
    out = []
    for pvi, mi, li in zip(pv, m, l):
        on = pvi * (1.0 / li)
        lse = mi + jnp.log(li)
        out.append((jnp.where(head0, on[:BAND], on[BAND:]),
                    jnp.where(head0, lse[:BAND], lse[BAND:])))
    return out


def _attn_kernel(x1_ref, x4_ref, x16_ref, out_ref, onat_s, lnat_s, oph_s, lph_s):
    seq = out_ref.shape[0]
    lane = lax.broadcasted_iota(jnp.int32, (1, LANES), 1)
    head0 = lane < HEAD_DIM
    qrow = lax.broadcasted_iota(jnp.int32, (BAND, 1), 0)
    kcol = lax.broadcasted_iota(jnp.int32, (1, 2 * BAND), 1)
    rel = BAND + qrow - kcol
    bias_rest = jnp.where((rel >= 0) & (rel <= BAND), 0.0, NEG_BIG).astype(F32)
    bias_first = jnp.where(kcol < BAND, NEG_BIG, bias_rest)
    consts = (head0, bias_first, bias_rest)
    group = ATTN_GROUP

    for bi, (d, x_ref) in enumerate(zip(DILATIONS, (x1_ref, x4_ref, x16_ref))):
        n_blocks = seq // (BAND * d)
        sub_len = seq // d
        for r in range(d):
            o_dst = onat_s.at[bi] if d == 1 else oph_s
            l_dst = lnat_s.at[bi] if d == 1 else lph_s

            def run(blocks, o_dst=o_dst, l_dst=l_dst, x_ref=x_ref, r=r):
                res = _attn_blocks(x_ref, r, blocks, consts)
                for n, (o, lse) in zip(blocks, res):
                    rows = (slice(n * BAND, (n + 1) * BAND) if isinstance(n, int)
                            else pl.ds(pl.multiple_of(n * BAND, BAND), BAND))
                    o_dst[rows, :] = o
                    l_dst[rows, :] = lse

            if n_blocks <= group:
                run(list(range(n_blocks)))
            else:
                def body(it, carry, run=run):
                    run([it * group + g for g in range(group)])
                    return carry
                lax.fori_loop(0, n_blocks // group, body, 0)
            if d > 1:
                onat_s[bi, pl.ds(r, sub_len, stride=d), :] = oph_s[0:sub_len, :]
                lnat_s[bi, pl.ds(r, sub_len, stride=d), :] = lph_s[0:sub_len, :]

    rows_c = 512
    for c in range(seq // rows_c):
        rows = slice(c * rows_c, (c + 1) * rows_c)
        l1, l2, l3 = lnat_s[0, rows, :], lnat_s[1, rows, :], lnat_s[2, rows, :]
        top = jnp.maximum(jnp.maximum(l1, l2), l3)
        w1, w2, w3 = jnp.exp(l1 - top), jnp.exp(l2 - top), jnp.exp(l3 - top)
        num = w1 * onat_s[0, rows, :] + w2 * onat_s[1, rows, :] + w3 * onat_s[2, rows, :]
        out_ref[rows, :] = (num / (w1 + w2 + w3)).astype(out_ref.dtype)


def _attention(x1, x4, x16, *, batch, seq):
    n_pairs = x1.shape[0]
    spec = lambda x: pl.BlockSpec((None, None) + x.shape[2:], lambda b, p: (p, b, 0, 0, 0, 0))
    sub4 = seq // DILATIONS[1]
    return pl.pallas_call(
        _attn_kernel,
        grid=(batch, n_pairs),
        in_specs=[spec(x1), spec(x4), spec(x16)],
        out_specs=pl.BlockSpec((None, seq, LANES), lambda b, p: (p, b, 0)),
        out_shape=jax.ShapeDtypeStruct((n_pairs, batch * seq, LANES), BF16),
        scratch_shapes=[pltpu.VMEM((3, seq, LANES), F32), pltpu.VMEM((3, seq, LANES), F32),
                        pltpu.VMEM((sub4, LANES), F32), pltpu.VMEM((sub4, LANES), F32)],
        compiler_params=pltpu.CompilerParams(
            dimension_semantics=("arbitrary", "arbitrary"), vmem_limit_bytes=VMEM_LIMIT),
        name="attention",
    )(x1, x4, x16)


def _block_diag(x, head0):
    zero = jnp.zeros_like(x)
    return jnp.concatenate([jnp.where(head0, x, zero), jnp.where(head0, zero, x)], axis=0)


def _chunk_consts():
    c = CHUNK
    trow = lax.broadcasted_iota(jnp.int32, (c, 1), 0)
    col3 = lax.broadcasted_iota(jnp.int32, (1, 3 * c), 1)
    tri3 = jnp.where((col3 & (c - 1)) <= trow, 1.0, 0.0).astype(BF16)
    lane_w = lax.broadcasted_iota(jnp.int32, (1, LANES), 1)
    scol = lane_w & (HEAD_DIM - 1)
    strict = scol < trow
    incl = scol <= trow
    eye_w = jnp.where(scol == trow, 1.0, 0.0).astype(F32)
    head0 = lane_w < HEAD_DIM
    rr = lax.broadcasted_iota(jnp.int32, (LANES, 1), 0)
    same_head = (rr // HEAD_DIM) == (lane_w // HEAD_DIM)
    levels = []
    s = 1
    while s < c:
        in_pair = (trow // (2 * s)) == (scol // (2 * s))
        levels.append(in_pair & ((trow // s) != (scol // s)) & strict)
        s *= 2
    return dict(tri3=tri3, strict=strict, incl=incl, eye_w=eye_w, head0=head0,
                same_head=same_head, levels=levels)


def _chunk_precompute(ins, consts):
    c = CHUNK
    head0 = consts["head0"]
    strict, incl, same_head = consts["strict"], consts["incl"], consts["same_head"]
    levels = consts["levels"]
    zero = jnp.zeros((c, LANES), F32)
    zero_sq = jnp.zeros((LANES, LANES), F32)

    def bd(x):
        return _block_diag(x, head0).astype(BF16)

    cums = []
    for (_, lw, _, _, _, _) in ins:
        hi = lw.astype(BF16)
        r1 = lw - hi.astype(F32)
        mid = r1.astype(BF16)
        lo = (r1 - mid.astype(F32)).astype(BF16)
        cums.append(_mm(consts["tri3"], jnp.concatenate([hi, mid, lo], axis=0)))
    ops = []
    for (r, lw, k, v, kk, b), cum in zip(ins, cums):
        cum_end = cum[c - 1:c, :]
        e_neg = jnp.exp(-cum)
        e_end = jnp.exp(cum_end - cum)
        ops.append(dict(kq=kk * jnp.exp(cum - lw), rq=r * jnp.exp(cum), kt=k * e_neg, bt=b * e_neg,
                        kh=(k * e_end).astype(BF16), bh=(b * e_end).astype(BF16),
                        g=jnp.exp(cum_end), v=v))
    a_all = [_nt(jnp.concatenate([o["kq"], o["rq"]], axis=0).astype(BF16),
                 jnp.concatenate([bd(o["bt"]), bd(o["kt"])], axis=0)) for o in ops]
    a_ab = [jnp.where(strict, a[:c, :LANES], zero) for a in a_all]
    a_rb = [jnp.where(incl, a[c:, :LANES], zero).astype(BF16) for a in a_all]
    av = [_mm(jnp.concatenate([jnp.where(strict, a[:c, LANES:], zero),
                               jnp.where(incl, a[c:, LANES:], zero)], axis=0).astype(BF16),
              bd(o["v"])) for a, o in zip(a_all, ops)]
    t_inv = [consts["eye_w"] - jnp.where(levels[0], a, zero) for a in a_ab]
    for mask in levels[1:]:
        x = [_mm(jnp.where(mask, a, zero).astype(BF16), bd(t)) for a, t in zip(a_ab, t_inv)]
        t_inv = [t - _mm(t.astype(BF16), bd(xi)) for t, xi in zip(t_inv, x)]
    twu = [_mm(t.astype(BF16), jnp.concatenate([bd(o["kq"]), bd(a[:c])], axis=1))
           for t, o, a in zip(t_inv, ops, av)]
    arb = [_mm(ar, jnp.concatenate([bd(x[:, :LANES]), bd(x[:, LANES:])], axis=1))
           for ar, x in zip(a_rb, twu)]
    out = []
    for o, a, x, y in zip(ops, av, twu, arb):
        w = x[:, :LANES].astype(BF16)
        u0 = x[:, LANES:]
        pm = -jnp.where(same_head, _tn(w, o["bh"]), zero_sq)
        dd = jnp.where(same_head,
                       _tn(jnp.concatenate([o["v"], -u0], axis=0).astype(BF16),
                           jnp.concatenate([o["kh"], o["bh"]], axis=0)), zero_sq)
        out.append((o["rq"] - y[:, :LANES], a[c:] - y[:, LANES:], pm, dd, o["g"]))
    return out


def _chunk_apply(pre, states):
    sb = [s.astype(BF16) for s in states]
    ys = [_nt(p[0].astype(BF16), s) + p[1] for p, s in zip(pre, sb)]
    new = [st * p[4] + _mm(s, p[2].astype(BF16)) + p[3] for p, s, st in zip(pre, sb, states)]
    return ys, new


def _rwkv_kernel(*refs, first_layer, tiles_per_seq, d_rwkv):
    if first_layer:
        (zr_ref, zl_ref, wl_ref, vec_ref, hs_ref, out_ref, vfirst_out_ref,
         r_s, lw_s, k_s, v_s, kk_s, b_s, y_s, state_s) = refs
        vfirst_ref = None
    else:
        (zr_ref, zl_ref, vfirst_ref, wl_ref, vec_ref, hs_ref, out_ref,
         r_s, lw_s, k_s, v_s, kk_s, b_s, y_s, state_s) = refs
    tc = zr_ref.shape[0]
    n_pairs = d_rwkv // LANES
    c = CHUNK

    @pl.when(pl.program_id(1) == 0)
    def _():
        state_s[...] = jnp.zeros_like(state_s)

    w0, a0, mv0, k_k, k_a, r_k, gn_w, gn_b = (vec_ref[j:j + 1, :] for j in range(8))
    hs = hs_ref[...]

    def head_sum(x):
        hi, lo = _split2(x)
        return _mm(hi, hs) + _mm(lo, hs)

    zl = zl_ref[...]
    lane = lax.broadcasted_iota(jnp.int32, (1, LORA_PAD), 1)
    o1 = LORA_DECAY
    o2 = o1 + LORA_AAA
    o3 = o2 + LORA_GATE
    act = jnp.where(lane < o1, jnp.tanh(zl),
                    jnp.where((lane >= o2) & (lane < o3), jax.nn.sigmoid(zl), zl))
    lora = _mm(act.astype(BF16), wl_ref[...])
    r = zr_ref[:, 0:d_rwkv]
    k = zr_ref[:, d_rwkv:2 * d_rwkv]
    v = zr_ref[:, 2 * d_rwkv:3 * d_rwkv]
    lw = -math.exp(-0.5) * jax.nn.sigmoid(w0 + lora[:, 0:d_rwkv])
    a = jax.nn.sigmoid(a0 + lora[:, d_rwkv:2 * d_rwkv])
    gate = lora[:, 2 * d_rwkv:3 * d_rwkv]
    if first_layer:
        vfirst_out_ref[...] = v
    else:
        vgate = jax.nn.sigmoid(mv0 + lora[:, 3 * d_rwkv:4 * d_rwkv])
        v = v + (vfirst_ref[...] - v) * vgate
    kk = k * k_k
    kk = kk / jnp.maximum(jnp.sqrt(head_sum(kk * kk)), 1e-12)
    k_mod = k * (1.0 + (a - 1.0) * k_a)
    r_s[...] = r
    lw_s[...] = lw
    k_s[...] = k_mod
    v_s[...] = v
    kk_s[...] = kk
    b_s[...] = kk * a

    consts = _chunk_consts()

    n_chunks = tc // c
    ins = []
    for ci in range(n_chunks):
        rows = slice(ci * c, (ci + 1) * c)
        for p in range(n_pairs):
            ls = slice(p * LANES, (p + 1) * LANES)
            ins.append(tuple(ref[rows, ls] for ref in (r_s, lw_s, k_s, v_s, kk_s, b_s)))
    pre = _chunk_precompute(ins, consts)
    states = [state_s[p] for p in range(n_pairs)]
    for ci in range(n_chunks):
        ys, states = _chunk_apply(pre[ci * n_pairs:(ci + 1) * n_pairs], states)
        for p in range(n_pairs):
            y_s[ci * c:(ci + 1) * c, p * LANES:(p + 1) * LANES] = ys[p]
    for p in range(n_pairs):
        state_s[p] = states[p]

    y = y_s[...]
    inv_n = 1.0 / HEAD_DIM
    mean = head_sum(y) * inv_n
    dlt = y - mean
    var = head_sum(dlt * dlt) * inv_n
    yn = dlt * lax.rsqrt(var + GN_EPS) * gn_w + gn_b
    bonus = head_sum(r * k_mod * r_k) * v
    out_ref[...] = ((yn + bonus) * gate).astype(out_ref.dtype)


def _rwkv(zr, zl, vfirst, wl, vecs, hs, *, batch, seq, tc=256):
    t = zr.shape[0]
    d_rwkv = zr.shape[1] // 3
    first_layer = vfirst is None
    tiles = seq // tc
    row = lambda b, i: (b * tiles + i, 0)
    const = lambda b, i: (0, 0)
    in_specs = [pl.BlockSpec((tc, 3 * d_rwkv), row), pl.BlockSpec((tc, LORA_PAD), row)]
    args = [zr, zl]
    if not first_layer:
        in_specs.append(pl.BlockSpec((tc, d_rwkv), row))
        args.append(vfirst)
    in_specs += [pl.BlockSpec(wl.shape, const), pl.BlockSpec(vecs.shape, const),
                 pl.BlockSpec(hs.shape, const)]
    args += [wl, vecs, hs]
    out_specs = [pl.BlockSpec((tc, d_rwkv), row)]
    out_shape = [jax.ShapeDtypeStruct((t, d_rwkv), BF16)]
    if first_layer:
        out_specs.append(pl.BlockSpec((tc, d_rwkv), row))
        out_shape.append(jax.ShapeDtypeStruct((t, d_rwkv), F32))
    kern = functools.partial(_rwkv_kernel, first_layer=first_layer, tiles_per_seq=tiles,
                             d_rwkv=d_rwkv)
    res = pl.pallas_call(
        kern,
        grid=(batch, tiles),
        in_specs=in_specs,
        out_specs=out_specs,
        out_shape=out_shape,
        scratch_shapes=[pltpu.VMEM((tc, d_rwkv), F32) for _ in range(7)]
        + [pltpu.VMEM((d_rwkv // LANES, LANES, LANES), F32)],
        compiler_params=pltpu.CompilerParams(
            dimension_semantics=("arbitrary", "arbitrary"), vmem_limit_bytes=VMEM_LIMIT),
        name="rwkv7_first" if first_layer else "rwkv7",
    )(*args)
    return (res[0], res[1]) if first_layer else (res[0], vfirst)


def _outproj_kernel(a_ref, rw_ref, x_ref, ga_ref, gp_ref, w_ref, out_ref):
    n_pairs = a_ref.shape[0]
    attn = jnp.concatenate([a_ref[p].astype(F32) for p in range(n_pairs)], axis=-1)
    attn = _rms(attn, ga_ref[...]).astype(BF16)
    d_attn = attn.shape[1]
    mixed = _mm(attn, w_ref[0:d_attn, :]) + _mm(rw_ref[...], w_ref[d_attn:, :])
    out_ref[...] = x_ref[...] + _rms(mixed, gp_ref[...])


def _outproj(attn, rw, x2, ga, gp, w, *, tm=512):
    t, d = x2.shape
    n_pairs = attn.shape[0]
    d_rwkv = rw.shape[1]
    row = lambda i: (i, 0)
    const = lambda i: (0, 0)
    return pl.pallas_call(
        _outproj_kernel,
        grid=(t // tm,),
        in_specs=[pl.BlockSpec((n_pairs, tm, LANES), lambda i: (0, i, 0)),
                  pl.BlockSpec((tm, d_rwkv), row), pl.BlockSpec((tm, d), row),
                  pl.BlockSpec((1, n_pairs * LANES), const), pl.BlockSpec((1, d), const),
                  pl.BlockSpec(w.shape, const)],
        out_specs=pl.BlockSpec((tm, d), row),
        out_shape=jax.ShapeDtypeStruct((t, d), F32),
        compiler_params=pltpu.CompilerParams(
            dimension_semantics=("arbitrary",), vmem_limit_bytes=VMEM_LIMIT),
        name="outproj",
    )(attn, rw, x2, ga, gp, w)


def _ffn_kernel(x_ref, g1_ref, g2_ref, wu_ref, wd_ref, out_ref, acc_ref, *, ff_chunk):
    x = x_ref[...]
    h = _rms(x, g1_ref[...]).astype(BF16)
    d_ff = wu_ref.shape[1]
    for c in range(d_ff // ff_chunk):
        cs = slice(c * ff_chunk, (c + 1) * ff_chunk)
        u = jnp.maximum(_mm(h, wu_ref[:, cs]), 0.0)
        part = _mm((u * u).astype(BF16), wd_ref[cs, :])
        if c == 0:
            acc_ref[...] = part
        else:
            acc_ref[...] += part
    out_ref[...] = x + _rms(acc_ref[...], g2_ref[...])


def _ffn(x2, g1, g2, wu, wd, *, tm=512, ff_chunk=1024):
    t, d = x2.shape
    row = lambda i: (i, 0)
    const = lambda i: (0, 0)
    return pl.pallas_call(
        functools.partial(_ffn_kernel, ff_chunk=ff_chunk),
        grid=(t // tm,),
        in_specs=[pl.BlockSpec((tm, d), row), pl.BlockSpec((1, d), const),
                  pl.BlockSpec((1, d), const), pl.BlockSpec(wu.shape, const),
                  pl.BlockSpec(wd.shape, const)],
        out_specs=pl.BlockSpec((tm, d), row),
        out_shape=jax.ShapeDtypeStruct((t, d), F32),
        scratch_shapes=[pltpu.VMEM((tm, d), F32)],
        compiler_params=pltpu.CompilerParams(
            dimension_semantics=("arbitrary",), vmem_limit_bytes=VMEM_LIMIT),
        name="ffn",
    )(x2, g1, g2, wu, wd)


def _layer_params(i, d_attn, d_rwkv, w_in_first, w_in_rest, mu_shift, mu_shift_mv,
                  decay_up, aaa_up, gate_up, mv_up):
    w = w_in_first if i == 0 else w_in_rest[i - 1]
    d = w.shape[0]
    n_first = w_in_first.shape[1]
    lora_used = LORA_DECAY + LORA_AAA + LORA_GATE + LORA_MV
    n_pairs = d_attn // LANES
    qkv = [w[:, j * d_attn + p * LANES:j * d_attn + (p + 1) * LANES]
           for p in range(n_pairs) for j in range(3)]
    pieces = qkv + [w[:, 3 * d_attn:n_first]]
    mus = [mu_shift[i]]
    if i == 0:
        pieces.append(jnp.zeros((d, LORA_MV), F32))
        mus.append(jnp.zeros((LORA_MV,), F32))
    else:
        pieces.append(w[:, n_first:])
        mus.append(mu_shift_mv[i - 1])
    pieces.append(jnp.zeros((d, LORA_PAD - lora_used), F32))
    mus.append(jnp.zeros((LORA_PAD - lora_used,), F32))
    w_r = jnp.concatenate(pieces, axis=1).astype(BF16)
    mu_r = jnp.concatenate(mus)[None, :]
    wl = jnp.zeros((LORA_PAD, 4 * d_rwkv), F32)
    o1 = LORA_DECAY
    o2 = o1 + LORA_AAA
    o3 = o2 + LORA_GATE
    wl = wl.at[0:o1, 0:d_rwkv].set(decay_up[i])
    wl = wl.at[o1:o2, d_rwkv:2 * d_rwkv].set(aaa_up[i])
    wl = wl.at[o2:o3, 2 * d_rwkv:3 * d_rwkv].set(gate_up[i])
    if i > 0:
        wl = wl.at[o3:o3 + LORA_MV, 3 * d_rwkv:4 * d_rwkv].set(mv_up[i - 1])
    return w_r, mu_r, wl.astype(BF16)


def kernel(x, norm_mix_pre, norm_mix_post, norm_ffn_pre, norm_ffn_post, w_in_first, w_in_rest,
           mu_shift, mu_shift_mv, attn_out_gain, decay_w0, decay_up, aaa_a0, aaa_up, mv_v0, mv_up,
           gate_up, k_k, k_a, r_k, gn_w, gn_b, w_out, w_ffn_up, w_ffn_down):
    batch, seq, d = x.shape
    depth = norm_mix_pre.shape[0]
    d_attn = attn_out_gain.shape[1]
    d_rwkv = decay_w0.shape[1]
    head = lax.broadcasted_iota(jnp.int32, (d_rwkv, d_rwkv), 0) // HEAD_DIM
    hs = (head == head.T).astype(BF16)
    x2 = x.reshape(batch * seq, d)
    vfirst = None
    for i in range(depth):
        w_r, mu_r, wl = _layer_params(i, d_attn, d_rwkv, w_in_first, w_in_rest, mu_shift,
                                      mu_shift_mv, decay_up, aaa_up, gate_up, mv_up)
        mv0 = mv_v0[i - 1] if i > 0 else jnp.zeros((d_rwkv,), F32)
        vecs = jnp.stack([decay_w0[i], aaa_a0[i], mv0, k_k[i], k_a[i], r_k[i].reshape(-1),
                          gn_w[i], gn_b[i]])
        x1, x4, x16, zr, zl = _inproj(x2, norm_mix_pre[i][None, :], w_r, mu_r, batch=batch, seq=seq)
        attn = _attention(x1, x4, x16, batch=batch, seq=seq)
        rw, vfirst = _rwkv(zr, zl, vfirst, wl, vecs, hs, batch=batch, seq=seq)
        x2 = _outproj(attn, rw, x2, attn_out_gain[i][None, :], norm_mix_post[i][None, :],
                      w_out[i].astype(BF16))
        x2 = _ffn(x2, norm_ffn_pre[i][None, :], norm_ffn_post[i][None, :],
                  w_ffn_up[i].astype(BF16), w_ffn_down[i].astype(BF16))
    return x2.reshape(batch, seq, d)
```

```python
import functools
import math

import jax
import jax.numpy as jnp
from jax import lax
from jax.experimental import pallas as pl
from jax.experimental.pallas import tpu as pltpu

F32 = jnp.float32
BF16 = jnp.bfloat16

HEAD_DIM = 64
LANES = 128
NORM_EPS = 1e-6
GN_EPS = 64e-5
DILATIONS = (1, 4, 16)
BAND = 128
QKV_W = 3 * LANES
INPROJ_TM = BAND * DILATIONS[1]
ATTN_GROUP = 4
LORA_DECAY, LORA_AAA, LORA_GATE, LORA_MV = 32, 32, 96, 32
LORA_PAD = 256
CHUNK = 64
RWKV_TC = 256
NEG_BIG = -1e30
VMEM_LIMIT = 56 * 1024 * 1024


def _nt(a, b):
    return lax.dot_general(a, b, (((1,), (1,)), ((), ())), preferred_element_type=F32)


def _tn(a, b):
    return lax.dot_general(a, b, (((0,), (0,)), ((), ())), preferred_element_type=F32)


def _mm(a, b):
    return jnp.dot(a, b, preferred_element_type=F32)


def _split2(x):
    hi = x.astype(BF16)
    lo = (x - hi.astype(F32)).astype(BF16)
    return hi, lo


def _rms(x, g):
    return x * lax.rsqrt(jnp.mean(x * x, axis=-1, keepdims=True) + NORM_EPS) * g


def _inproj_kernel(x_ref, g_ref, w_ref, mu_ref, x1_ref, x4_ref, x16_ref, zr_ref, zl_ref,
                   carry_ref, zs_ref, *, seq_tiles, n_pairs, d_rwkv3):
    i = pl.program_id(0)
    tm = x_ref.shape[0]
    h = _rms(x_ref[...], g_ref[...]).astype(BF16)
    for p in range(n_pairs):
        z = _mm(h, w_ref[:, p * QKV_W:(p + 1) * QKV_W])
        x1_ref[p, :, 0] = z.astype(BF16).reshape(tm // BAND, BAND, QKV_W)
        for j in range(QKV_W // LANES):
            ls = slice(j * LANES, (j + 1) * LANES)
            zs_ref[p, j] = z[:, ls]
            for d, ref in ((DILATIONS[1], x4_ref), (DILATIONS[2], x16_ref)):
                for r in range(d):
                    ref[p, r, :, ls] = zs_ref[p, j, pl.ds(r, tm // d, stride=d), :].astype(BF16)
    d_attn3 = n_pairs * QKV_W
    first_tile = (i % seq_tiles) == 0
    row0 = lax.broadcasted_iota(jnp.int32, (tm, 1), 0) == 0
    n_shift = d_rwkv3 + LORA_PAD
    cw = 512
    c0 = 0
    while c0 < n_shift:
        w = min(cw, n_shift - c0)
        z = _mm(h, w_ref[:, d_attn3 + c0:d_attn3 + c0 + w])
        carry = jnp.where(first_tile, 0.0, carry_ref[:, c0:c0 + w])
        prev = jnp.where(row0, carry, pltpu.roll(z, 1, 0))
        carry_ref[:, c0:c0 + w] = z[tm - 1:tm, :]
        zs = z + (prev - z) * mu_ref[:, c0:c0 + w]
        if c0 < d_rwkv3:
            zr_ref[:, c0:c0 + w] = zs
        else:
            zl_ref[...] = zs
        c0 += w


def _inproj(x2, g, w, mu, *, batch, seq):
    tm = INPROJ_TM
    t, d = x2.shape
    nc = w.shape[1]
    n_pairs = (d // 2) // LANES
    d_attn3 = n_pairs * QKV_W
    d_rwkv3 = nc - d_attn3 - LORA_PAD
    seq_tiles = seq // tm
    d4, d16 = DILATIONS[1], DILATIONS[2]
    assert tm == BAND * d4 and (BAND * d16) % tm == 0 and seq % (BAND * d16) == 0
    sub16 = BAND * d16 // tm
    kern = functools.partial(_inproj_kernel, seq_tiles=seq_tiles, n_pairs=n_pairs, d_rwkv3=d_rwkv3)
    return pl.pallas_call(
        kern,
        grid=(t // tm,),
        in_specs=[
            pl.BlockSpec((tm, d), lambda i: (i, 0)),
            pl.BlockSpec((1, d), lambda i: (0, 0)),
            pl.BlockSpec((d, nc), lambda i: (0, 0)),
            pl.BlockSpec((1, nc - d_attn3), lambda i: (0, 0)),
        ],
        out_specs=[
            pl.BlockSpec((n_pairs, None, tm // BAND, 1, BAND, QKV_W),
                         lambda i: (0, i // seq_tiles, i % seq_tiles, 0, 0, 0)),
            pl.BlockSpec((n_pairs, None, None, d4, BAND, QKV_W),
                         lambda i: (0, i // seq_tiles, i % seq_tiles, 0, 0, 0)),
            pl.BlockSpec((n_pairs, None, None, d16, tm // d16, QKV_W),
                         lambda i: (0, i // seq_tiles, (i % seq_tiles) // sub16, 0,
                                    (i % seq_tiles) % sub16, 0)),
            pl.BlockSpec((tm, d_rwkv3), lambda i: (i, 0)),
            pl.BlockSpec((tm, LORA_PAD), lambda i: (i, 0)),
        ],
        out_shape=[
            jax.ShapeDtypeStruct((n_pairs, batch, seq // BAND, 1, BAND, QKV_W), BF16),
            jax.ShapeDtypeStruct((n_pairs, batch, seq // (BAND * d4), d4, BAND, QKV_W), BF16),
            jax.ShapeDtypeStruct((n_pairs, batch, seq // (BAND * d16), d16, BAND, QKV_W), BF16),
            jax.ShapeDtypeStruct((t, d_rwkv3), F32),
            jax.ShapeDtypeStruct((t, LORA_PAD), F32),
        ],
        scratch_shapes=[pltpu.VMEM((1, nc - d_attn3), F32),
                        pltpu.VMEM((n_pairs, QKV_W // LANES, tm, LANES), F32)],
        compiler_params=pltpu.CompilerParams(
            dimension_semantics=("arbitrary",), vmem_limit_bytes=VMEM_LIMIT),
        name="inproj",
    )(x2, g, w, mu)


def _attn_blocks(x_ref, probs, consts):
    head0, bias_first, bias_rest = consts
    scale = 1.0 / math.sqrt(HEAD_DIM)
    ops = []
    for r, n in probs:
        prev = max(n - 1, 0) if isinstance(n, int) else jnp.maximum(n - 1, 0)
        q = x_ref[n, r, :, 0:LANES] * scale
        kw = jnp.concatenate([x_ref[prev, r, :, LANES:2 * LANES], x_ref[n, r, :, LANES:2 * LANES]],
                             axis=0)
        vw = jnp.concatenate([x_ref[prev, r, :, 2 * LANES:], x_ref[n, r, :, 2 * LANES:]], axis=0)
        bias = jnp.where(n == 0, bias_first, bias_rest)
        ops.append((q, kw, vw, bias))
    s = [_nt(_block_diag(q, head0), kw) for q, kw, _, _ in ops]
    s = [jnp.concatenate([si[:BAND] + o[3], si[BAND:] + o[3]], axis=0) for si, o in zip(s, ops)]
    m = [jnp.max(si, axis=-1, keepdims=True) for si in s]
    p = [jnp.exp(si - mi) for si, mi in zip(s, m)]
    l = [jnp.sum(pi, axis=-1, keepdims=True) for pi in p]
    pv = [_mm(pi.astype(BF16), o[2]) for pi, o in zip(p, ops)]
    out = []
    for pvi, mi, li in zip(pv, m, l):
        on = pvi * (1.0 / li)
        lse = mi + jnp.log(li)
        out.append((jnp.where(head0, on[:BAND], on[BAND:]),
                    jnp.where(head0, lse[:BAND], lse[BAND:])))
    return out


def _attn_kernel(x1_ref, x4_ref, x16_ref, out_ref, onat_s, lnat_s, oph_s, lph_s):
    seq = out_ref.shape[0]
    lane = lax.broadcasted_iota(jnp.int32, (1, LANES), 1)
    head0 = lane < HEAD_DIM
    qrow = lax.broadcasted_iota(jnp.int32, (BAND, 1), 0)
    kcol = lax.broadcasted_iota(jnp.int32, (1, 2 * BAND), 1)
    rel = BAND + qrow - kcol
    bias_rest = jnp.where((rel >= 0) & (rel <= BAND), 0.0, NEG_BIG).astype(F32)
    bias_first = jnp.where(kcol < BAND, NEG_BIG, bias_rest)
    consts = (head0, bias_first, bias_rest)
    group = ATTN_GROUP

    for bi, (d, x_ref) in enumerate(zip(DILATIONS, (x1_ref, x4_ref, x16_ref))):
        n_blocks = seq // (BAND * d)
        sub_len = seq // d
        per_phase = min(group, n_blocks)
        n_phases = max(1, group // n_blocks)
        o_dst = onat_s.at[bi] if d == 1 else oph_s
        l_dst = lnat_s.at[bi] if d == 1 else lph_s
        for r0 in range(0, d, n_phases):

            def run(n0, r0=r0, x_ref=x_ref, o_dst=o_dst, l_dst=l_dst, sub_len=sub_len,
                    per_phase=per_phase, n_phases=n_phases):
                probs = [(r0 + j, n0 + g) for j in range(n_phases) for g in range(per_phase)]
                res = _attn_blocks(x_ref, probs, consts)
                for (r, n), (o, lse) in zip(probs, res):
                    base = (r - r0) * sub_len
                    rows = (slice(base + n * BAND, base + (n + 1) * BAND) if isinstance(n, int)
                            else pl.ds(pl.multiple_of(base + n * BAND, BAND), BAND))
                    o_dst[rows, :] = o
                    l_dst[rows, :] = lse

            if n_blocks == per_phase:
                run(0)
            else:
                def body(it, carry, run=run, per_phase=per_phase):
                    run(it * per_phase)
                    return carry
                lax.fori_loop(0, n_blocks // per_phase, body, 0)
            if d > 1:
                for j in range(n_phases):
                    staged = slice(j * sub_len, (j + 1) * sub_len)
                    onat_s[bi, pl.ds(r0 + j, sub_len, stride=d), :] = oph_s[staged, :]
                    lnat_s[bi, pl.ds(r0 + j, sub_len, stride=d), :] = lph_s[staged, :]

    rows_c = 512
    for c in range(seq // rows_c):
        rows = slice(c * rows_c, (c + 1) * rows_c)
        l1, l2, l3 = lnat_s[0, rows, :], lnat_s[1, rows, :], lnat_s[2, rows, :]
        top = jnp.maximum(jnp.maximum(l1, l2), l3)
        w1, w2, w3 = jnp.exp(l1 - top), jnp.exp(l2 - top), jnp.exp(l3 - top)
        num = w1 * onat_s[0, rows, :] + w2 * onat_s[1, rows, :] + w3 * onat_s[2, rows, :]
        out_ref[rows, :] = (num / (w1 + w2 + w3)).astype(out_ref.dtype)


def _attention(x1, x4, x16, *, batch, seq):
    n_pairs = x1.shape[0]
    spec = lambda x: pl.BlockSpec((None, None) + x.shape[2:], lambda b, p: (p, b, 0, 0, 0, 0))
    sub4 = seq // DILATIONS[1]
    return pl.pallas_call(
        _attn_kernel,
        grid=(batch, n_pairs),
        in_specs=[spec(x1), spec(x4), spec(x16)],
        out_specs=pl.BlockSpec((None, seq, LANES), lambda b, p: (p, b, 0)),
        out_shape=jax.ShapeDtypeStruct((n_pairs, batch * seq, LANES), BF16),
        scratch_shapes=[pltpu.VMEM((3, seq, LANES), F32), pltpu.VMEM((3, seq, LANES), F32),
                        pltpu.VMEM((sub4, LANES), F32), pltpu.VMEM((sub4, LANES), F32)],
        compiler_params=pltpu.CompilerParams(
            dimension_semantics=("arbitrary", "arbitrary"), vmem_limit_bytes=VMEM_LIMIT),
        name="attention",
    )(x1, x4, x16)


def _block_diag(x, head0):
    zero = jnp.zeros_like(x)
    return jnp.concatenate([jnp.where(head0, x, zero), jnp.where(head0, zero, x)], axis=0)


def _chunk_consts():
    c = CHUNK
    trow = lax.broadcasted_iota(jnp.int32, (c, 1), 0)
    col3 = lax.broadcasted_iota(jnp.int32, (1, 3 * c), 1)
    tri3 = jnp.where((col3 & (c - 1)) <= trow, 1.0, 0.0).astype(BF16)
    lane_w = lax.broadcasted_iota(jnp.int32, (1, LANES), 1)
    scol = lane_w & (HEAD_DIM - 1)
    strict = scol < trow
    incl = scol <= trow
    eye_w = jnp.where(scol == trow, 1.0, 0.0).astype(F32)
    head0 = lane_w < HEAD_DIM
    rr = lax.broadcasted_iota(jnp.int32, (LANES, 1), 0)
    same_head = (rr // HEAD_DIM) == (lane_w // HEAD_DIM)
    levels = []
    s = 1
    while s < c:
        in_pair = (trow // (2 * s)) == (scol // (2 * s))
        levels.append(in_pair & ((trow // s) != (scol // s)) & strict)
        s *= 2
    return dict(tri3=tri3, strict=strict, incl=incl, eye_w=eye_w, head0=head0,
                same_head=same_head, levels=levels)


def _chunk_precompute(ins, consts):
    c = CHUNK
    head0 = consts["head0"]
    strict, incl, same_head = consts["strict"], consts["incl"], consts["same_head"]
    levels = consts["levels"]
    zero = jnp.zeros((c, LANES), F32)
    zero_sq = jnp.zeros((LANES, LANES), F32)

    def bd(x):
        return _block_diag(x, head0).astype(BF16)

    cums = []
    for (_, lw, _, _, _, _) in ins:
        hi = lw.astype(BF16)
        r1 = lw - hi.astype(F32)
        mid = r1.astype(BF16)
        lo = (r1 - mid.astype(F32)).astype(BF16)
        cums.append(_mm(consts["tri3"], jnp.concatenate([hi, mid, lo], axis=0)))
    ops = []
    for (r, lw, k, v, kk, b), cum in zip(ins, cums):
        cum_end = cum[c - 1:c, :]
        e_neg = jnp.exp(-cum)
        e_end = jnp.exp(cum_end - cum)
        ops.append(dict(kq=kk * jnp.exp(cum - lw), rq=r * jnp.exp(cum), kt=k * e_neg, bt=b * e_neg,
                        kh=(k * e_end).astype(BF16), bh=(b * e_end).astype(BF16),
                        g=jnp.exp(cum_end), v=v))
    a_all = [_nt(jnp.concatenate([o["kq"], o["rq"]], axis=0).astype(BF16),
                 jnp.concatenate([bd(o["bt"]), bd(o["kt"])], axis=0)) for o in ops]
    a_ab = [jnp.where(strict, a[:c, :LANES], zero) for a in a_all]
    a_rb = [jnp.where(incl, a[c:, :LANES], zero).astype(BF16) for a in a_all]
    av = [_mm(jnp.concatenate([jnp.where(strict, a[:c, LANES:], zero),
                               jnp.where(incl, a[c:, LANES:], zero)], axis=0).astype(BF16),
              bd(o["v"])) for a, o in zip(a_all, ops)]
    t_inv = [consts["eye_w"] - jnp.where(levels[0], a, zero) for a in a_ab]
    for mask in levels[1:]:
        x = [_mm(jnp.where(mask, a, zero).astype(BF16), bd(t)) for a, t in zip(a_ab, t_inv)]
        t_inv = [t - _mm(t.astype(BF16), bd(xi)) for t, xi in zip(t_inv, x)]
    twu = [_mm(t.astype(BF16), jnp.concatenate([bd(o["kq"]), bd(a[:c])], axis=1))
           for t, o, a in zip(t_inv, ops, av)]
    arb = [_mm(ar, jnp.concatenate([bd(x[:, :LANES]), bd(x[:, LANES:])], axis=1))
           for ar, x in zip(a_rb, twu)]
    out = []
    for o, a, x, y in zip(ops, av, twu, arb):
        w = x[:, :LANES].astype(BF16)
        u0 = x[:, LANES:]
        pm = -jnp.where(same_head, _tn(w, o["bh"]), zero_sq)
        dd = jnp.where(same_head,
                       _tn(jnp.concatenate([o["v"], -u0], axis=0).astype(BF16),
                           jnp.concatenate([o["kh"], o["bh"]], axis=0)), zero_sq)
        out.append((o["rq"] - y[:, :LANES], a[c:] - y[:, LANES:], pm, dd, o["g"]))
    return out


def _chunk_apply(pre, states):
    sb = [s.astype(BF16) for s in states]
    ys = [_nt(p[0].astype(BF16), s) + p[1] for p, s in zip(pre, sb)]
    new = [st * p[4] + _mm(s, p[2].astype(BF16)) + p[3] for p, s, st in zip(pre, sb, states)]
    return ys, new


def _rwkv_kernel(*refs, first_layer, tiles_per_seq, d_rwkv):
    if first_layer:
        (zr_ref, zl_ref, wl_ref, vec_ref, out_ref, vfirst_out_ref,
         r_s, lw_s, k_s, v_s, kk_s, b_s, y_s, state_s) = refs
        vfirst_ref = None
    else:
        (zr_ref, zl_ref, vfirst_ref, wl_ref, vec_ref, out_ref,
         r_s, lw_s, k_s, v_s, kk_s, b_s, y_s, state_s) = refs
    tc = zr_ref.shape[0]
    n_pairs = d_rwkv // LANES
    c = CHUNK

    @pl.when(pl.program_id(1) == 0)
    def _():
        state_s[...] = jnp.zeros_like(state_s)

    w0, a0, mv0, k_k, k_a, r_k, gn_w, gn_b = (vec_ref[j:j + 1, :] for j in range(8))
    hrow = lax.broadcasted_iota(jnp.int32, (2 * LANES, 1), 0) & (LANES - 1)
    hcol = lax.broadcasted_iota(jnp.int32, (1, LANES), 1)
    ones2 = jnp.where((hrow // HEAD_DIM) == (hcol // HEAD_DIM), 1.0, 0.0).astype(BF16)

    def head_sum(x):
        hi, lo = _split2(x)
        parts = []
        for p in range(n_pairs):
            ls = slice(p * LANES, (p + 1) * LANES)
            parts.append(_mm(jnp.concatenate([hi[:, ls], lo[:, ls]], axis=1), ones2))
        return jnp.concatenate(parts, axis=1)

    zl = zl_ref[...]
    lane = lax.broadcasted_iota(jnp.int32, (1, LORA_PAD), 1)
    o1 = LORA_DECAY
    o2 = o1 + LORA_AAA
    o3 = o2 + LORA_GATE
    act = jnp.where(lane < o1, jnp.tanh(zl),
                    jnp.where((lane >= o2) & (lane < o3), jax.nn.sigmoid(zl), zl))
    lora = _mm(act.astype(BF16), wl_ref[...])
    r = zr_ref[:, 0:d_rwkv]
    k = zr_ref[:, d_rwkv:2 * d_rwkv]
    v = zr_ref[:, 2 * d_rwkv:3 * d_rwkv]
    lw = -math.exp(-0.5) * jax.nn.sigmoid(w0 + lora[:, 0:d_rwkv])
    a = jax.nn.sigmoid(a0 + lora[:, d_rwkv:2 * d_rwkv])
    gate = lora[:, 2 * d_rwkv:3 * d_rwkv]
    if first_layer:
        vfirst_out_ref[...] = v
    else:
        vgate = jax.nn.sigmoid(mv0 + lora[:, 3 * d_rwkv:4 * d_rwkv])
        v = v + (vfirst_ref[...] - v) * vgate
    kk = k * k_k
    kk = kk / jnp.maximum(jnp.sqrt(head_sum(kk * kk)), 1e-12)
    k_mod = k * (1.0 + (a - 1.0) * k_a)
    r_s[...] = r
    lw_s[...] = lw
    k_s[...] = k_mod
    v_s[...] = v
    kk_s[...] = kk
    b_s[...] = kk * a

    consts = _chunk_consts()

    n_chunks = tc // c
    ins = []
    for ci in range(n_chunks):
        rows = slice(ci * c, (ci + 1) * c)
        for p in range(n_pairs):
            ls = slice(p * LANES, (p + 1) * LANES)
            ins.append(tuple(ref[rows, ls] for ref in (r_s, lw_s, k_s, v_s, kk_s, b_s)))
    pre = _chunk_precompute(ins, consts)
    states = [state_s[p] for p in range(n_pairs)]
    for ci in range(n_chunks):
        ys, states = _chunk_apply(pre[ci * n_pairs:(ci + 1) * n_pairs], states)
        for p in range(n_pairs):
            y_s[ci * c:(ci + 1) * c, p * LANES:(p + 1) * LANES] = ys[p]
    for p in range(n_pairs):
        state_s[p] = states[p]

    y = y_s[...]
    inv_n = 1.0 / HEAD_DIM
    mean = head_sum(y) * inv_n
    dlt = y - mean
    var = head_sum(dlt * dlt) * inv_n
    yn = dlt * lax.rsqrt(var + GN_EPS) * gn_w + gn_b
    bonus = head_sum(r * k_mod * r_k) * v
    out_ref[...] = ((yn + bonus) * gate).astype(out_ref.dtype)


def _rwkv(zr, zl, vfirst, wl, vecs, *, batch, seq, tc=RWKV_TC):
    t = zr.shape[0]
    d_rwkv = zr.shape[1] // 3
    first_layer = vfirst is None
    tiles = seq // tc
    row = lambda b, i: (b * tiles + i, 0)
    const = lambda b, i: (0, 0)
    in_specs = [pl.BlockSpec((tc, 3 * d_rwkv), row), pl.BlockSpec((tc, LORA_PAD), row)]
    args = [zr, zl]
    if not first_layer:
        in_specs.append(pl.BlockSpec((tc, d_rwkv), row))
        args.append(vfirst)
    in_specs += [pl.BlockSpec(wl.shape, const), pl.BlockSpec(vecs.shape, const)]
    args += [wl, vecs]
    out_specs = [pl.BlockSpec((tc, d_rwkv), row)]
    out_shape = [jax.ShapeDtypeStruct((t, d_rwkv), BF16)]
    if first_layer:
        out_specs.append(pl.BlockSpec((tc, d_rwkv), row))
        out_shape.append(jax.ShapeDtypeStruct((t, d_rwkv), F32))
    kern = functools.partial(_rwkv_kernel, first_layer=first_layer, tiles_per_seq=tiles,
                             d_rwkv=d_rwkv)
    res = pl.pallas_call(
        kern,
        grid=(batch, tiles),
        in_specs=in_specs,
        out_specs=out_specs,
        out_shape=out_shape,
        scratch_shapes=[pltpu.VMEM((tc, d_rwkv), F32) for _ in range(7)]
        + [pltpu.VMEM((d_rwkv // LANES, LANES, LANES), F32)],
        compiler_params=pltpu.CompilerParams(
            dimension_semantics=("arbitrary", "arbitrary"), vmem_limit_bytes=VMEM_LIMIT),
        name="rwkv7_first" if first_layer else "rwkv7",
    )(*args)
    return (res[0], res[1]) if first_layer else (res[0], vfirst)


def _outproj_kernel(a_ref, rw_ref, x_ref, ga_ref, gp_ref, w_ref, out_ref):
    n_pairs = a_ref.shape[0]
    attn = jnp.concatenate([a_ref[p].astype(F32) for p in range(n_pairs)], axis=-1)
    attn = _rms(attn, ga_ref[...]).astype(BF16)
    d_attn = attn.shape[1]
    mixed = _mm(attn, w_ref[0:d_attn, :]) + _mm(rw_ref[...], w_ref[d_attn:, :])
    out_ref[...] = x_ref[...] + _rms(mixed, gp_ref[...])


def _outproj(attn, rw, x2, ga, gp, w, *, tm=512):
    t, d = x2.shape
    n_pairs = attn.shape[0]
    d_rwkv = rw.shape[1]
    row = lambda i: (i, 0)
    const = lambda i: (0, 0)
    return pl.pallas_call(
        _outproj_kernel,
        grid=(t // tm,),
        in_specs=[pl.BlockSpec((n_pairs, tm, LANES), lambda i: (0, i, 0)),
                  pl.BlockSpec((tm, d_rwkv), row), pl.BlockSpec((tm, d), row),
                  pl.BlockSpec((1, n_pairs * LANES), const), pl.BlockSpec((1, d), const),
                  pl.BlockSpec(w.shape, const)],
        out_specs=pl.BlockSpec((tm, d), row),
        out_shape=jax.ShapeDtypeStruct((t, d), F32),
        compiler_params=pltpu.CompilerParams(
            dimension_semantics=("arbitrary",), vmem_limit_bytes=VMEM_LIMIT),
        name="outproj",
    )(attn, rw, x2, ga, gp, w)


def _ffn_kernel(x_ref, g1_ref, g2_ref, wu_ref, wd_ref, out_ref, acc_ref, *, ff_chunk):
    x = x_ref[...]
    h = _rms(x, g1_ref[...]).astype(BF16)
    d_ff = wu_ref.shape[1]
    for c in range(d_ff // ff_chunk):
        cs = slice(c * ff_chunk, (c + 1) * ff_chunk)
        u = jnp.maximum(_mm(h, wu_ref[:, cs]), 0.0)
        part = _mm((u * u).astype(BF16), wd_ref[cs, :])
        if c == 0:
            acc_ref[...] = part
        else:
            acc_ref[...] += part
    out_ref[...] = x + _rms(acc_ref[...], g2_ref[...])


def _ffn(x2, g1, g2, wu, wd, *, tm=512, ff_chunk=1024):
    t, d = x2.shape
    row = lambda i: (i, 0)
    const = lambda i: (0, 0)
    return pl.pallas_call(
        functools.partial(_ffn_kernel, ff_chunk=ff_chunk),
        grid=(t // tm,),
        in_specs=[pl.BlockSpec((tm, d), row), pl.BlockSpec((1, d), const),
                  pl.BlockSpec((1, d), const), pl.BlockSpec(wu.shape, const),
                  pl.BlockSpec(wd.shape, const)],
        out_specs=pl.BlockSpec((tm, d), row),
        out_shape=jax.ShapeDtypeStruct((t, d), F32),
        scratch_shapes=[pltpu.VMEM((tm, d), F32)],
        compiler_params=pltpu.CompilerParams(
            dimension_semantics=("arbitrary",), vmem_limit_bytes=VMEM_LIMIT),
        name="ffn",
    )(x2, g1, g2, wu, wd)


def _layer_params(i, d_attn, d_rwkv, w_in_first, w_in_rest, mu_shift, mu_shift_mv,
                  decay_up, aaa_up, gate_up, mv_up):
    w = w_in_first if i == 0 else w_in_rest[i - 1]
    d = w.shape[0]
    n_first = w_in_first.shape[1]
    lora_used = LORA_DECAY + LORA_AAA + LORA_GATE + LORA_MV
    n_pairs = d_attn // LANES
    qkv = [w[:, j * d_attn + p * LANES:j * d_attn + (p + 1) * LANES]
           for p in range(n_pairs) for j in range(3)]
    pieces = qkv + [w[:, 3 * d_attn:n_first]]
    mus = [mu_shift[i]]
    if i == 0:
        pieces.append(jnp.zeros((d, LORA_MV), F32))
        mus.append(jnp.zeros((LORA_MV,), F32))
    else:
        pieces.append(w[:, n_first:])
        mus.append(mu_shift_mv[i - 1])
    pieces.append(jnp.zeros((d, LORA_PAD - lora_used), F32))
    mus.append(jnp.zeros((LORA_PAD - lora_used,), F32))
    w_r = jnp.concatenate(pieces, axis=1).astype(BF16)
    mu_r = jnp.concatenate(mus)[None, :]
    wl = jnp.zeros((LORA_PAD, 4 * d_rwkv), F32)
    o1 = LORA_DECAY
    o2 = o1 + LORA_AAA
    o3 = o2 + LORA_GATE
    wl = wl.at[0:o1, 0:d_rwkv].set(decay_up[i])
    wl = wl.at[o1:o2, d_rwkv:2 * d_rwkv].set(aaa_up[i])
    wl = wl.at[o2:o3, 2 * d_rwkv:3 * d_rwkv].set(gate_up[i])
    if i > 0:
        wl = wl.at[o3:o3 + LORA_MV, 3 * d_rwkv:4 * d_rwkv].set(mv_up[i - 1])
    return w_r, mu_r, wl.astype(BF16)


def kernel(x, norm_mix_pre, norm_mix_post, norm_ffn_pre, norm_ffn_post, w_in_first, w_in_rest,
           mu_shift, mu_shift_mv, attn_out_gain, decay_w0, decay_up, aaa_a0, aaa_up, mv_v0, mv_up,
           gate_up, k_k, k_a, r_k, gn_w, gn_b, w_out, w_ffn_up, w_ffn_down):
    batch, seq, d = x.shape
    depth = norm_mix_pre.shape[0]
    d_attn = attn_out_gain.shape[1]
    d_rwkv = decay_w0.shape[1]
    x2 = x.reshape(batch * seq, d)
    vfirst = None
    for i in range(depth):
        w_r, mu_r, wl = _layer_params(i, d_attn, d_rwkv, w_in_first, w_in_rest, mu_shift,
                                      mu_shift_mv, decay_up, aaa_up, gate_up, mv_up)
        mv0 = mv_v0[i - 1] if i > 0 else jnp.zeros((d_rwkv,), F32)
        vecs = jnp.stack([decay_w0[i], aaa_a0[i], mv0, k_k[i], k_a[i], r_k[i].reshape(-1),
                          gn_w[i], gn_b[i]])
        x1, x4, x16, zr, zl = _inproj(x2, norm_mix_pre[i][None, :], w_r, mu_r, batch=batch, seq=seq)
        attn = _attention(x1, x4, x16, batch=batch, seq=seq)
        rw, vfirst = _rwkv(zr, zl, vfirst, wl, vecs, batch=batch, seq=seq)
        x2 = _outproj(attn, rw, x2, attn_out_gain[i][None, :], norm_mix_post[i][None, :],
                      w_out[i].astype(BF16))
        x2 = _ffn(x2, norm_ffn_pre[i][None, :], norm_ffn_post[i][None, :],
                  w_ffn_up[i].astype(BF16), w_ffn_down[i].astype(BF16))
    return x2.reshape(batch, seq, d)
```

```python
import functools
import math

import jax
import jax.numpy as jnp
from jax import lax
from jax.experimental import pallas as pl
from jax.experimental.pallas import tpu as pltpu

F32 = jnp.float32
BF16 = jnp.bfloat16

HEAD_DIM = 64
LANES = 128
NORM_EPS = 1e-6
GN_EPS = 64e-5
DILATIONS = (1, 4, 16)
BAND = 128
QKV_W = 3 * LANES
INPROJ_TM = BAND * DILATIONS[1]
ATTN_GROUP = 4
LORA_DECAY, LORA_AAA, LORA_GATE, LORA_MV = 32, 32, 96, 32
LORA_PAD = 256
CHUNK = 64
RWKV_TC = 256
NEG_BIG = -1e30
VMEM_LIMIT = 56 * 1024 * 1024


def _nt(a, b):
    return lax.dot_general(a, b, (((1,), (1,)), ((), ())), preferred_element_type=F32)


def _tn(a, b):
    return lax.dot_general(a, b, (((0,), (0,)), ((), ())), preferred_element_type=F32)


def _mm(a, b):
    return jnp.dot(a, b, preferred_element_type=F32)


def _split2(x):
    hi = x.astype(BF16)
    lo = (x - hi.astype(F32)).astype(BF16)
    return hi, lo


def _rms(x, g):
    return x * lax.rsqrt(jnp.mean(x * x, axis=-1, keepdims=True) + NORM_EPS) * g


def _inproj_kernel(x_ref, g_ref, w_ref, mu_ref, x1_ref, x4_ref, x16_ref, zr_ref, zl_ref,
                   carry_ref, zs_ref, z4_ref, *, seq_tiles, n_pairs, d_rwkv3):
    i = pl.program_id(0)
    tm = x_ref.shape[0]
    h = _rms(x_ref[...], g_ref[...]).astype(BF16)
    for p in range(n_pairs):
        z = _mm(h, w_ref[:, p * QKV_W:(p + 1) * QKV_W])
        x1_ref[p, :, 0] = z.astype(BF16).reshape(tm // BAND, BAND, QKV_W)
        f = DILATIONS[1]
        for j in range(QKV_W // LANES):
            ls = slice(j * LANES, (j + 1) * LANES)
            zs_ref[p, j] = z[:, ls]
            for b in range(f):
                z4 = zs_ref[p, j, pl.ds(b, tm // f, stride=f), :]
                x4_ref[p, b, :, ls] = z4.astype(BF16)
                z4_ref[p, j, b] = z4
                for a in range(f):
                    z16 = z4_ref[p, j, b, pl.ds(a, tm // (f * f), stride=f), :]
                    x16_ref[p, f * a + b, :, ls] = z16.astype(BF16)
    d_attn3 = n_pairs * QKV_W
    first_tile = (i % seq_tiles) == 0
    row0 = lax.broadcasted_iota(jnp.int32, (tm, 1), 0) == 0
    n_shift = d_rwkv3 + LORA_PAD
    cw = 512
    c0 = 0
    while c0 < n_shift:
        w = min(cw, n_shift - c0)
        z = _mm(h, w_ref[:, d_attn3 + c0:d_attn3 + c0 + w])
        carry = jnp.where(first_tile, 0.0, carry_ref[:, c0:c0 + w])
        prev = jnp.where(row0, carry, pltpu.roll(z, 1, 0))
        carry_ref[:, c0:c0 + w] = z[tm - 1:tm, :]
        zs = z + (prev - z) * mu_ref[:, c0:c0 + w]
        if c0 < d_rwkv3:
            zr_ref[:, c0:c0 + w] = zs
        else:
            zl_ref[...] = zs
        c0 += w


def _inproj(x2, g, w, mu, *, batch, seq):
    tm = INPROJ_TM
    t, d = x2.shape
    nc = w.shape[1]
    n_pairs = (d // 2) // LANES
    d_attn3 = n_pairs * QKV_W
    d_rwkv3 = nc - d_attn3 - LORA_PAD
    seq_tiles = seq // tm
    d4, d16 = DILATIONS[1], DILATIONS[2]
    assert tm == BAND * d4 and d16 == d4 * d4 and seq % (BAND * d16) == 0
    sub16 = BAND * d16 // tm
    kern = functools.partial(_inproj_kernel, seq_tiles=seq_tiles, n_pairs=n_pairs, d_rwkv3=d_rwkv3)
    return pl.pallas_call(
        kern,
        grid=(t // tm,),
        in_specs=[
            pl.BlockSpec((tm, d), lambda i: (i, 0)),
            pl.BlockSpec((1, d), lambda i: (0, 0)),
            pl.BlockSpec((d, nc), lambda i: (0, 0)),
            pl.BlockSpec((1, nc - d_attn3), lambda i: (0, 0)),
        ],
        out_specs=[
            pl.BlockSpec((n_pairs, None, tm // BAND, 1, BAND, QKV_W),
                         lambda i: (0, i // seq_tiles, i % seq_tiles, 0, 0, 0)),
            pl.BlockSpec((n_pairs, None, None, d4, BAND, QKV_W),
                         lambda i: (0, i // seq_tiles, i % seq_tiles, 0, 0, 0)),
            pl.BlockSpec((n_pairs, None, None, d16, tm // d16, QKV_W),
                         lambda i: (0, i // seq_tiles, (i % seq_tiles) // sub16, 0,
                                    (i % seq_tiles) % sub16, 0)),
            pl.BlockSpec((tm, d_rwkv3), lambda i: (i, 0)),
            pl.BlockSpec((tm, LORA_PAD), lambda i: (i, 0)),
        ],
        out_shape=[
            jax.ShapeDtypeStruct((n_pairs, batch, seq // BAND, 1, BAND, QKV_W), BF16),
            jax.ShapeDtypeStruct((n_pairs, batch, seq // (BAND * d4), d4, BAND, QKV_W), BF16),
            jax.ShapeDtypeStruct((n_pairs, batch, seq // (BAND * d16), d16, BAND, QKV_W), BF16),
            jax.ShapeDtypeStruct((t, d_rwkv3), F32),
            jax.ShapeDtypeStruct((t, LORA_PAD), F32),
        ],
        scratch_shapes=[pltpu.VMEM((1, nc - d_attn3), F32),
                        pltpu.VMEM((n_pairs, QKV_W // LANES, tm, LANES), F32),
                        pltpu.VMEM((n_pairs, QKV_W // LANES, d4, tm // d4, LANES), F32)],
        compiler_params=pltpu.CompilerParams(
            dimension_semantics=("arbitrary",), vmem_limit_bytes=VMEM_LIMIT),
        name="inproj",
    )(x2, g, w, mu)


def _attn_scores(x_ref, probs, head0):
    scale = 1.0 / math.sqrt(HEAD_DIM)
    out = []
    for r, n in probs:
        prev = jnp.maximum(n - 1, 0)
        q = x_ref[n, r, :, 0:LANES] * scale
        kw = jnp.concatenate([x_ref[prev, r, :, LANES:2 * LANES], x_ref[n, r, :, LANES:2 * LANES]],
                             axis=0)
        out.append(_nt(_block_diag(q, head0), kw))
    return out


def _attn_softmax_pv(x_ref, probs, scores, consts):
    head0, bias_first, bias_rest = consts
    s = []
    for (r, n), si in zip(probs, scores):
        bias = jnp.where(n == 0, bias_first, bias_rest)
        s.append(jnp.concatenate([si[:BAND] + bias, si[BAND:] + bias], axis=0))
    m = [jnp.max(si, axis=-1, keepdims=True) for si in s]
    p = [jnp.exp(si - mi).astype(BF16) for si, mi in zip(s, m)]
    ones = jnp.ones((2 * BAND, LANES), BF16)
    pv = []
    for (r, n), pi in zip(probs, p):
        prev = jnp.maximum(n - 1, 0)
        vw = jnp.concatenate([x_ref[prev, r, :, 2 * LANES:], x_ref[n, r, :, 2 * LANES:]], axis=0)
        pv.append(_mm(pi, jnp.concatenate([vw, ones], axis=1)))
    return [tuple(jnp.where(head0, t[:BAND], t[BAND:]) for t in (pvi[:, :LANES], mi, pvi[:, LANES:]))
            for pvi, mi in zip(pv, m)]


def _attn_kernel(x1_ref, x4_ref, x16_ref, out_ref, nat_s, ph_s, mid_s, s_scr):
    seq = out_ref.shape[0]
    lane = lax.broadcasted_iota(jnp.int32, (1, LANES), 1)
    head0 = lane < HEAD_DIM
    qrow = lax.broadcasted_iota(jnp.int32, (BAND, 1), 0)
    kcol = lax.broadcasted_iota(jnp.int32, (1, 2 * BAND), 1)
    rel = BAND + qrow - kcol
    bias_rest = jnp.where((rel >= 0) & (rel <= BAND), 0.0, NEG_BIG).astype(F32)
    bias_first = jnp.where(kcol < BAND, NEG_BIG, bias_rest)
    consts = (head0, bias_first, bias_rest)
    group = ATTN_GROUP
    n_groups = seq // (BAND * group)

    for bi, (d, x_ref) in enumerate(zip(DILATIONS, (x1_ref, x4_ref, x16_ref))):
        n_blocks = seq // (BAND * d)
        shift = n_blocks.bit_length() - 1
        sub_len = seq // d
        dst = nat_s.at[bi] if d == 1 else ph_s

        def probs_of(g, shift=shift, n_blocks=n_blocks):
            js = [g * group + i for i in range(group)]
            return [(lax.shift_right_logical(j, shift), j & (n_blocks - 1)) for j in js], js

        probs0, _ = probs_of(jnp.int32(0))
        for i, si in enumerate(_attn_scores(x_ref, probs0, head0)):
            s_scr[i] = si

        def body(g, carry, x_ref=x_ref, dst=dst, probs_of=probs_of):
            nxt, _ = probs_of(jnp.minimum(g + 1, n_groups - 1))
            s_next = _attn_scores(x_ref, nxt, head0)
            cur, js = probs_of(g)
            res = _attn_softmax_pv(x_ref, cur, [s_scr[i] for i in range(group)], consts)
            for j, aml in zip(js, res):
                rows = pl.ds(pl.multiple_of(j * BAND, BAND), BAND)
                for q, t in enumerate(aml):
                    dst[q, rows, :] = t
            for i, si in enumerate(s_next):
                s_scr[i] = si
            return carry

        lax.fori_loop(0, n_groups, body, 0)
        f = DILATIONS[1]
        if d == f:
            for r in range(d):
                staged = slice(r * sub_len, (r + 1) * sub_len)
                for q in range(3):
                    nat_s[bi, q, pl.ds(r, sub_len, stride=f), :] = ph_s[q, staged, :]
        elif d == f * f:
            for q in range(3):
                for b in range(f):
                    for a in range(f):
                        staged = slice((f * a + b) * sub_len, (f * a + b + 1) * sub_len)
                        mid_s[pl.ds(b * f * sub_len + a, sub_len, stride=f), :] = ph_s[q, staged, :]
                for b in range(f):
                    rows = slice(b * f * sub_len, (b + 1) * f * sub_len)
                    nat_s[bi, q, pl.ds(b, f * sub_len, stride=f), :] = mid_s[rows, :]

    rows_c = 512
    for c in range(seq // rows_c):
        rows = slice(c * rows_c, (c + 1) * rows_c)
        ms = [nat_s[bi, 1, rows, :] for bi in range(3)]
        top = jnp.maximum(jnp.maximum(ms[0], ms[1]), ms[2])
        es = [jnp.exp(mb - top) for mb in ms]
        num = sum(e * nat_s[bi, 0, rows, :] for bi, e in enumerate(es))
        den = sum(e * nat_s[bi, 2, rows, :] for bi, e in enumerate(es))
        out_ref[rows, :] = (num / den).astype(out_ref.dtype)


def _attention(x1, x4, x16, *, batch, seq):
    n_pairs = x1.shape[0]
    spec = lambda x: pl.BlockSpec((None, None) + x.shape[2:], lambda b, p: (p, b, 0, 0, 0, 0))
    return pl.pallas_call(
        _attn_kernel,
        grid=(batch, n_pairs),
        in_specs=[spec(x1), spec(x4), spec(x16)],
        out_specs=pl.BlockSpec((None, seq, LANES), lambda b, p: (p, b, 0)),
        out_shape=jax.ShapeDtypeStruct((n_pairs, batch * seq, LANES), BF16),
        scratch_shapes=[pltpu.VMEM((3, 3, seq, LANES), F32), pltpu.VMEM((3, seq, LANES), F32),
                        pltpu.VMEM((seq, LANES), F32),
                        pltpu.VMEM((ATTN_GROUP, 2 * BAND, 2 * BAND), F32)],
        compiler_params=pltpu.CompilerParams(
            dimension_semantics=("arbitrary", "arbitrary"), vmem_limit_bytes=VMEM_LIMIT),
        name="attention",
    )(x1, x4, x16)


def _block_diag(x, head0):
    zero = jnp.zeros_like(x)
    return jnp.concatenate([jnp.where(head0, x, zero), jnp.where(head0, zero, x)], axis=0)


def _chunk_consts():
    c = CHUNK
    trow = lax.broadcasted_iota(jnp.int32, (c, 1), 0)
    col3 = lax.broadcasted_iota(jnp.int32, (1, 3 * c), 1)
    tri3 = jnp.where((col3 & (c - 1)) <= trow, 1.0, 0.0).astype(BF16)
    lane_w = lax.broadcasted_iota(jnp.int32, (1, LANES), 1)
    scol = lane_w & (HEAD_DIM - 1)
    strict = scol < trow
    incl = scol <= trow
    eye_w = jnp.where(scol == trow, 1.0, 0.0).astype(F32)
    head0 = lane_w < HEAD_DIM
    rr = lax.broadcasted_iota(jnp.int32, (LANES, 1), 0)
    same_head = (rr // HEAD_DIM) == (lane_w // HEAD_DIM)
    levels = []
    s = 1
    while s < c:
        in_pair = (trow // (2 * s)) == (scol // (2 * s))
        levels.append(in_pair & ((trow // s) != (scol // s)) & strict)
        s *= 2
    return dict(tri3=tri3, strict=strict, incl=incl, eye_w=eye_w, head0=head0,
                same_head=same_head, levels=levels)


def _chunk_precompute(ins, consts):
    c = CHUNK
    head0 = consts["head0"]
    strict, incl, same_head = consts["strict"], consts["incl"], consts["same_head"]
    levels = consts["levels"]
    zero = jnp.zeros((c, LANES), F32)
    zero_sq = jnp.zeros((LANES, LANES), F32)

    def bd(x):
        return _block_diag(x, head0).astype(BF16)

    cums = []
    for (_, lw, _, _, _, _) in ins:
        hi = lw.astype(BF16)
        r1 = lw - hi.astype(F32)
        mid = r1.astype(BF16)
        lo = (r1 - mid.astype(F32)).astype(BF16)
        cums.append(_mm(consts["tri3"], jnp.concatenate([hi, mid, lo], axis=0)))
    ops = []
    for (r, lw, k, v, kk, b), cum in zip(ins, cums):
        cum_end = cum[c - 1:c, :]
        e_neg = jnp.exp(-cum)
        e_end = jnp.exp(cum_end - cum)
        ops.append(dict(kq=kk * jnp.exp(cum - lw), rq=r * jnp.exp(cum), kt=k * e_neg, bt=b * e_neg,
                        kh=(k * e_end).astype(BF16), bh=(b * e_end).astype(BF16),
                        g=jnp.exp(cum_end), v=v))
    a_all = [_nt(jnp.concatenate([o["kq"], o["rq"]], axis=0).astype(BF16),
                 jnp.concatenate([bd(o["bt"]), bd(o["kt"])], axis=0)) for o in ops]
    a_ab = [jnp.where(strict, a[:c, :LANES], zero) for a in a_all]
    a_rb = [jnp.where(incl, a[c:, :LANES], zero).astype(BF16) for a in a_all]
    av = [_mm(jnp.concatenate([jnp.where(strict, a[:c, LANES:], zero),
                               jnp.where(incl, a[c:, LANES:], zero)], axis=0).astype(BF16),
              bd(o["v"])) for a, o in zip(a_all, ops)]
    t_inv = [consts["eye_w"] - jnp.where(levels[0], a, zero) for a in a_ab]
    for mask in levels[1:]:
        x = [_mm(jnp.where(mask, a, zero).astype(BF16), bd(t)) for a, t in zip(a_ab, t_inv)]
        t_inv = [t - _mm(t.astype(BF16), bd(xi)) for t, xi in zip(t_inv, x)]
    twu = [_mm(t.astype(BF16), jnp.concatenate([bd(o["kq"]), bd(a[:c])], axis=1))
           for t, o, a in zip(t_inv, ops, av)]
    arb = [_mm(ar, jnp.concatenate([bd(x[:, :LANES]), bd(x[:, LANES:])], axis=1))
           for ar, x in zip(a_rb, twu)]
    out = []
    for o, a, x, y in zip(ops, av, twu, arb):
        w = x[:, :LANES].astype(BF16)
        u0 = x[:, LANES:]
        pm = -jnp.where(same_head, _tn(w, o["bh"]), zero_sq)
        dd = jnp.where(same_head,
                       _tn(jnp.concatenate([o["v"], -u0], axis=0).astype(BF16),
                           jnp.concatenate([o["kh"], o["bh"]], axis=0)), zero_sq)
        out.append((o["rq"] - y[:, :LANES], a[c:] - y[:, LANES:], pm, dd, o["g"]))
    return out


def _chunk_apply(pre, states):
    sb = [s.astype(BF16) for s in states]
    ys = [_nt(p[0].astype(BF16), s) + p[1] for p, s in zip(pre, sb)]
    new = [st * p[4] + _mm(s, p[2].astype(BF16)) + p[3] for p, s, st in zip(pre, sb, states)]
    return ys, new


def _rwkv_kernel(*refs, first_layer, tiles_per_seq, d_rwkv):
    if first_layer:
        (zr_ref, zl_ref, wl_ref, vec_ref, out_ref, vfirst_out_ref,
         r_s, lw_s, k_s, v_s, kk_s, b_s, y_s, state_s) = refs
        vfirst_ref = None
    else:
        (zr_ref, zl_ref, vfirst_ref, wl_ref, vec_ref, out_ref,
         r_s, lw_s, k_s, v_s, kk_s, b_s, y_s, state_s) = refs
    tc = zr_ref.shape[0]
    n_pairs = d_rwkv // LANES
    c = CHUNK

    @pl.when(pl.program_id(1) == 0)
    def _():
        state_s[...] = jnp.zeros_like(state_s)

    w0, a0, mv0, k_k, k_a, r_k, gn_w, gn_b = (vec_ref[j:j + 1, :] for j in range(8))
    hrow = lax.broadcasted_iota(jnp.int32, (2 * LANES, 1), 0) & (LANES - 1)
    hcol = lax.broadcasted_iota(jnp.int32, (1, LANES), 1)
    ones2 = jnp.where((hrow // HEAD_DIM) == (hcol // HEAD_DIM), 1.0, 0.0).astype(BF16)

    def head_sum(x):
        hi, lo = _split2(x)
        parts = []
        for p in range(n_pairs):
            ls = slice(p * LANES, (p + 1) * LANES)
            parts.append(_mm(jnp.concatenate([hi[:, ls], lo[:, ls]], axis=1), ones2))
        return jnp.concatenate(parts, axis=1)

    zl = zl_ref[...]
    lane = lax.broadcasted_iota(jnp.int32, (1, LORA_PAD), 1)
    o1 = LORA_DECAY
    o2 = o1 + LORA_AAA
    o3 = o2 + LORA_GATE
    act = jnp.where(lane < o1, jnp.tanh(zl),
                    jnp.where((lane >= o2) & (lane < o3), jax.nn.sigmoid(zl), zl))
    lora = _mm(act.astype(BF16), wl_ref[...])
    r = zr_ref[:, 0:d_rwkv]
    k = zr_ref[:, d_rwkv:2 * d_rwkv]
    v = zr_ref[:, 2 * d_rwkv:3 * d_rwkv]
    lw = -math.exp(-0.5) * jax.nn.sigmoid(w0 + lora[:, 0:d_rwkv])
    a = jax.nn.sigmoid(a0 + lora[:, d_rwkv:2 * d_rwkv])
    gate = lora[:, 2 * d_rwkv:3 * d_rwkv]
    if first_layer:
        vfirst_out_ref[...] = v
    else:
        vgate = jax.nn.sigmoid(mv0 + lora[:, 3 * d_rwkv:4 * d_rwkv])
        v = v + (vfirst_ref[...] - v) * vgate
    kk = k * k_k
    kk = kk / jnp.maximum(jnp.sqrt(head_sum(kk * kk)), 1e-12)
    k_mod = k * (1.0 + (a - 1.0) * k_a)
    r_s[...] = r
    lw_s[...] = lw
    k_s[...] = k_mod
    v_s[...] = v
    kk_s[...] = kk
    b_s[...] = kk * a

    consts = _chunk_consts()

    n_chunks = tc // c
    ins = []
    for ci in range(n_chunks):
        rows = slice(ci * c, (ci + 1) * c)
        for p in range(n_pairs):
            ls = slice(p * LANES, (p + 1) * LANES)
            ins.append(tuple(ref[rows, ls] for ref in (r_s, lw_s, k_s, v_s, kk_s, b_s)))
    pre = _chunk_precompute(ins, consts)
    states = [state_s[p] for p in range(n_pairs)]
    for ci in range(n_chunks):
        ys, states = _chunk_apply(pre[ci * n_pairs:(ci + 1) * n_pairs], states)
        for p in range(n_pairs):
            y_s[ci * c:(ci + 1) * c, p * LANES:(p + 1) * LANES] = ys[p]
    for p in range(n_pairs):
        state_s[p] = states[p]

    y = y_s[...]
    inv_n = 1.0 / HEAD_DIM
    mean = head_sum(y) * inv_n
    dlt = y - mean
    var = head_sum(dlt * dlt) * inv_n
    yn = dlt * lax.rsqrt(var + GN_EPS) * gn_w + gn_b
    bonus = head_sum(r * k_mod * r_k) * v
    out_ref[...] = ((yn + bonus) * gate).astype(out_ref.dtype)


def _rwkv(zr, zl, vfirst, wl, vecs, *, batch, seq, tc=RWKV_TC):
    t = zr.shape[0]
    d_rwkv = zr.shape[1] // 3
    first_layer = vfirst is None
    tiles = seq // tc
    row = lambda b, i: (b * tiles + i, 0)
    const = lambda b, i: (0, 0)
    in_specs = [pl.BlockSpec((tc, 3 * d_rwkv), row), pl.BlockSpec((tc, LORA_PAD), row)]
    args = [zr, zl]
    if not first_layer:
        in_specs.append(pl.BlockSpec((tc, d_rwkv), row))
        args.append(vfirst)
    in_specs += [pl.BlockSpec(wl.shape, const), pl.BlockSpec(vecs.shape, const)]
    args += [wl, vecs]
    out_specs = [pl.BlockSpec((tc, d_rwkv), row)]
    out_shape = [jax.ShapeDtypeStruct((t, d_rwkv), BF16)]
    if first_layer:
        out_specs.append(pl.BlockSpec((tc, d_rwkv), row))
        out_shape.append(jax.ShapeDtypeStruct((t, d_rwkv), F32))
    kern = functools.partial(_rwkv_kernel, first_layer=first_layer, tiles_per_seq=tiles,
                             d_rwkv=d_rwkv)
    res = pl.pallas_call(
        kern,
        grid=(batch, tiles),
        in_specs=in_specs,
        out_specs=out_specs,
        out_shape=out_shape,
        scratch_shapes=[pltpu.VMEM((tc, d_rwkv), F32) for _ in range(7)]
        + [pltpu.VMEM((d_rwkv // LANES, LANES, LANES), F32)],
        compiler_params=pltpu.CompilerParams(
            dimension_semantics=("arbitrary", "arbitrary"), vmem_limit_bytes=VMEM_LIMIT),
        name="rwkv7_first" if first_layer else "rwkv7",
    )(*args)
    return (res[0], res[1]) if first_layer else (res[0], vfirst)


def _mix_ffn_kernel(a_ref, rw_ref, x_ref, gains_ref, ga_ref, wo_ref, wu_ref, wd_ref, out_ref,
                    acc_ref, *, ff_chunk):
    n_pairs = a_ref.shape[0]
    g_post, g_pre, g_ffn = (gains_ref[j:j + 1, :] for j in range(3))
    attn = jnp.concatenate([a_ref[p].astype(F32) for p in range(n_pairs)], axis=-1)
    attn = _rms(attn, ga_ref[...]).astype(BF16)
    d_attn = attn.shape[1]
    mixed = _mm(attn, wo_ref[0:d_attn, :]) + _mm(rw_ref[...], wo_ref[d_attn:, :])
    x1 = x_ref[...] + _rms(mixed, g_post)
    h = _rms(x1, g_pre).astype(BF16)
    d_ff = wu_ref.shape[1]
    for c in range(d_ff // ff_chunk):
        cs = slice(c * ff_chunk, (c + 1) * ff_chunk)
        u = jnp.maximum(_mm(h, wu_ref[:, cs]), 0.0)
        part = _mm((u * u).astype(BF16), wd_ref[cs, :])
        if c == 0:
            acc_ref[...] = part
        else:
            acc_ref[...] += part
    out_ref[...] = x1 + _rms(acc_ref[...], g_ffn)


def _mix_ffn(attn, rw, x2, gains, ga, wo, wu, wd, *, tm=512, ff_chunk=1024):
    t, d = x2.shape
    n_pairs = attn.shape[0]
    d_rwkv = rw.shape[1]
    row = lambda i: (i, 0)
    const = lambda i: (0, 0)
    resident = lambda shape: pl.BlockSpec(shape, const, pipeline_mode=pl.Buffered(1))
    return pl.pallas_call(
        functools.partial(_mix_ffn_kernel, ff_chunk=ff_chunk),
        grid=(t // tm,),
        in_specs=[pl.BlockSpec((n_pairs, tm, LANES), lambda i: (0, i, 0)),
                  pl.BlockSpec((tm, d_rwkv), row), pl.BlockSpec((tm, d), row),
                  pl.BlockSpec(gains.shape, const), pl.BlockSpec(ga.shape, const),
                  resident(wo.shape), resident(wu.shape), resident(wd.shape)],
        out_specs=pl.BlockSpec((tm, d), row),
        out_shape=jax.ShapeDtypeStruct((t, d), F32),
        scratch_shapes=[pltpu.VMEM((tm, d), F32)],
        compiler_params=pltpu.CompilerParams(
            dimension_semantics=("arbitrary",), vmem_limit_bytes=VMEM_LIMIT),
        name="mix_ffn",
    )(attn, rw, x2, gains, ga, wo, wu, wd)


def _layer_params(i, d_attn, d_rwkv, w_in_first, w_in_rest, mu_shift, mu_shift_mv,
                  decay_up, aaa_up, gate_up, mv_up):
    w = w_in_first if i == 0 else w_in_rest[i - 1]
    d = w.shape[0]
    n_first = w_in_first.shape[1]
    lora_used = LORA_DECAY + LORA_AAA + LORA_GATE + LORA_MV
    n_pairs = d_attn // LANES
    qkv = [w[:, j * d_attn + p * LANES:j * d_attn + (p + 1) * LANES]
           for p in range(n_pairs) for j in range(3)]
    pieces = qkv + [w[:, 3 * d_attn:n_first]]
    mus = [mu_shift[i]]
    if i == 0:
        pieces.append(jnp.zeros((d, LORA_MV), F32))
        mus.append(jnp.zeros((LORA_MV,), F32))
    else:
        pieces.append(w[:, n_first:])
        mus.append(mu_shift_mv[i - 1])
    pieces.append(jnp.zeros((d, LORA_PAD - lora_used), F32))
    mus.append(jnp.zeros((LORA_PAD - lora_used,), F32))
    w_r = jnp.concatenate(pieces, axis=1).astype(BF16)
    mu_r = jnp.concatenate(mus)[None, :]
    wl = jnp.zeros((LORA_PAD, 4 * d_rwkv), F32)
    o1 = LORA_DECAY
    o2 = o1 + LORA_AAA
    o3 = o2 + LORA_GATE
    wl = wl.at[0:o1, 0:d_rwkv].set(decay_up[i])
    wl = wl.at[o1:o2, d_rwkv:2 * d_rwkv].set(aaa_up[i])
    wl = wl.at[o2:o3, 2 * d_rwkv:3 * d_rwkv].set(gate_up[i])
    if i > 0:
        wl = wl.at[o3:o3 + LORA_MV, 3 * d_rwkv:4 * d_rwkv].set(mv_up[i - 1])
    return w_r, mu_r, wl.astype(BF16)


def kernel(x, norm_mix_pre, norm_mix_post, norm_ffn_pre, norm_ffn_post, w_in_first, w_in_rest,
           mu_shift, mu_shift_mv, attn_out_gain, decay_w0, decay_up, aaa_a0, aaa_up, mv_v0, mv_up,
           gate_up, k_k, k_a, r_k, gn_w, gn_b, w_out, w_ffn_up, w_ffn_down):
    batch, seq, d = x.shape
    depth = norm_mix_pre.shape[0]
    d_attn = attn_out_gain.shape[1]
    d_rwkv = decay_w0.shape[1]
    x2 = x.reshape(batch * seq, d)
    vfirst = None
    for i in range(depth):
        w_r, mu_r, wl = _layer_params(i, d_attn, d_rwkv, w_in_first, w_in_rest, mu_shift,
                                      mu_shift_mv, decay_up, aaa_up, gate_up, mv_up)
        mv0 = mv_v0[i - 1] if i > 0 else jnp.zeros((d_rwkv,), F32)
        vecs = jnp.stack([decay_w0[i], aaa_a0[i], mv0, k_k[i], k_a[i], r_k[i].reshape(-1),
                          gn_w[i], gn_b[i]])
        x1, x4, x16, zr, zl = _inproj(x2, norm_mix_pre[i][None, :], w_r, mu_r, batch=batch, seq=seq)
        attn = _attention(x1, x4, x16, batch=batch, seq=seq)
        rw, vfirst = _rwkv(zr, zl, vfirst, wl, vecs, batch=batch, seq=seq)
        gains = jnp.stack([norm_mix_post[i], norm_ffn_pre[i], norm_ffn_post[i]])
        x2 = _mix_ffn(attn, rw, x2, gains, attn_out_gain[i][None, :], w_out[i].astype(BF16),
                      w_ffn_up[i].astype(BF16), w_ffn_down[i].astype(BF16))
    return x2.reshape(batch, seq, d)
```

```python
import functools
import math

import jax
import jax.numpy as jnp
from jax import lax
from jax.experimental import pallas as pl
from jax.experimental.pallas import tpu as pltpu

F32 = jnp.float32
BF16 = jnp.bfloat16

HEAD_DIM = 64
LANES = 128
NORM_EPS = 1e-6
GN_EPS = 64e-5
DILATIONS = (1, 4, 16)
BAND = 128
QKV_W = 3 * LANES
INPROJ_TM = BAND * DILATIONS[1]
ATTN_GROUP = 4
LORA_DECAY, LORA_AAA, LORA_GATE, LORA_MV = 32, 32, 96, 32
LORA_PAD = 256
CHUNK = 64
RWKV_TC = 512
RWKV_SUB = 256
NEG_BIG = -1e30
VMEM_LIMIT = 56 * 1024 * 1024


def _nt(a, b):
    return lax.dot_general(a, b, (((1,), (1,)), ((), ())), preferred_element_type=F32)


def _tn(a, b):
    return lax.dot_general(a, b, (((0,), (0,)), ((), ())), preferred_element_type=F32)


def _mm(a, b):
    return jnp.dot(a, b, preferred_element_type=F32)


def _rms(x, g):
    return x * lax.rsqrt(jnp.mean(x * x, axis=-1, keepdims=True) + NORM_EPS) * g


def _inproj_kernel(x_ref, g_ref, w_ref, mu_ref, x1_ref, x4_ref, x16_ref, zr_ref, zl_ref,
                   carry_ref, zs_ref, z4_ref, *, seq_tiles, n_pairs, d_rwkv3):
    i = pl.program_id(0)
    tm = x_ref.shape[0]
    h = _rms(x_ref[...], g_ref[...]).astype(BF16)
    for p in range(n_pairs):
        if p % 2 == 0:
            zz = _mm(h, w_ref[:, p * QKV_W:(p + 2) * QKV_W])
        z = zz[:, (p % 2) * QKV_W:(p % 2 + 1) * QKV_W]
        x1_ref[p, :, 0] = z.astype(BF16).reshape(tm // BAND, BAND, QKV_W)
        f = DILATIONS[1]
        for j in range(QKV_W // LANES):
            ls = slice(j * LANES, (j + 1) * LANES)
            zs_ref[p, j] = z[:, ls]
            for b in range(f):
                z4 = zs_ref[p, j, pl.ds(b, tm // f, stride=f), :]
                x4_ref[p, b, :, ls] = z4.astype(BF16)
                z4_ref[p, j, b] = z4
                for a in range(f):
                    z16 = z4_ref[p, j, b, pl.ds(a, tm // (f * f), stride=f), :]
                    x16_ref[p, f * a + b, :, ls] = z16.astype(BF16)
    d_attn3 = n_pairs * QKV_W
    first_tile = (i % seq_tiles) == 0
    row0 = lax.broadcasted_iota(jnp.int32, (tm, 1), 0) == 0
    n_shift = d_rwkv3 + LORA_PAD
    cw = 512
    c0 = 0
    while c0 < n_shift:
        w = min(cw, n_shift - c0)
        z = _mm(h, w_ref[:, d_attn3 + c0:d_attn3 + c0 + w])
        carry = jnp.where(first_tile, 0.0, carry_ref[:, c0:c0 + w])
        prev = jnp.where(row0, carry, pltpu.roll(z, 1, 0))
        carry_ref[:, c0:c0 + w] = z[tm - 1:tm, :]
        zs = z + (prev - z) * mu_ref[:, c0:c0 + w]
        if c0 < d_rwkv3:
            zr_ref[:, c0:c0 + w] = zs
        else:
            zl_ref[...] = zs
        c0 += w


def _inproj(x2, g, w, mu, *, batch, seq):
    tm = INPROJ_TM
    t, d = x2.shape
    nc = w.shape[1]
    n_pairs = (d // 2) // LANES
    d_attn3 = n_pairs * QKV_W
    d_rwkv3 = nc - d_attn3 - LORA_PAD
    seq_tiles = seq // tm
    d4, d16 = DILATIONS[1], DILATIONS[2]
    assert tm == BAND * d4 and d16 == d4 * d4 and seq % (BAND * d16) == 0
    sub16 = BAND * d16 // tm
    kern = functools.partial(_inproj_kernel, seq_tiles=seq_tiles, n_pairs=n_pairs, d_rwkv3=d_rwkv3)
    return pl.pallas_call(
        kern,
        grid=(t // tm,),
        in_specs=[
            pl.BlockSpec((tm, d), lambda i: (i, 0)),
            pl.BlockSpec((1, d), lambda i: (0, 0)),
            pl.BlockSpec((d, nc), lambda i: (0, 0)),
            pl.BlockSpec((1, nc - d_attn3), lambda i: (0, 0)),
        ],
        out_specs=[
            pl.BlockSpec((n_pairs, None, tm // BAND, 1, BAND, QKV_W),
                         lambda i: (0, i // seq_tiles, i % seq_tiles, 0, 0, 0)),
            pl.BlockSpec((n_pairs, None, None, d4, BAND, QKV_W),
                         lambda i: (0, i // seq_tiles, i % seq_tiles, 0, 0, 0)),
            pl.BlockSpec((n_pairs, None, None, d16, tm // d16, QKV_W),
                         lambda i: (0, i // seq_tiles, (i % seq_tiles) // sub16, 0,
                                    (i % seq_tiles) % sub16, 0)),
            pl.BlockSpec((tm, d_rwkv3), lambda i: (i, 0)),
            pl.BlockSpec((tm, LORA_PAD), lambda i: (i, 0)),
        ],
        out_shape=[
            jax.ShapeDtypeStruct((n_pairs, batch, seq // BAND, 1, BAND, QKV_W), BF16),
            jax.ShapeDtypeStruct((n_pairs, batch, seq // (BAND * d4), d4, BAND, QKV_W), BF16),
            jax.ShapeDtypeStruct((n_pairs, batch, seq // (BAND * d16), d16, BAND, QKV_W), BF16),
            jax.ShapeDtypeStruct((t, d_rwkv3), F32),
            jax.ShapeDtypeStruct((t, LORA_PAD), F32),
        ],
        scratch_shapes=[pltpu.VMEM((1, nc - d_attn3), F32),
                        pltpu.VMEM((n_pairs, QKV_W // LANES, tm, LANES), F32),
                        pltpu.VMEM((n_pairs, QKV_W // LANES, d4, tm // d4, LANES), F32)],
        compiler_params=pltpu.CompilerParams(
            dimension_semantics=("arbitrary",), vmem_limit_bytes=VMEM_LIMIT),
        name="inproj",
    )(x2, g, w, mu)


def _attn_scores(x_ref, probs, head0):
    scale = 1.0 / math.sqrt(HEAD_DIM)
    out = []
    for r, n in probs:
        prev = jnp.maximum(n - 1, 0)
        q = x_ref[n, r, :, 0:LANES] * scale
        kw = jnp.concatenate([x_ref[prev, r, :, LANES:2 * LANES], x_ref[n, r, :, LANES:2 * LANES]],
                             axis=0)
        out.append(_nt(_block_diag(q, head0), kw))
    return out


def _attn_softmax_pv(x_ref, probs, scores, consts):
    head0, bias_first, bias_rest = consts
    s = []
    for (r, n), si in zip(probs, scores):
        bias = jnp.where(n == 0, bias_first, bias_rest)
        s.append(jnp.concatenate([si[:BAND] + bias, si[BAND:] + bias], axis=0))
    m = [jnp.max(si, axis=-1, keepdims=True) for si in s]
    p = [jnp.exp(si - mi).astype(BF16) for si, mi in zip(s, m)]
    ones = jnp.ones((2 * BAND, LANES), BF16)
    pv = []
    for (r, n), pi in zip(probs, p):
        prev = jnp.maximum(n - 1, 0)
        vw = jnp.concatenate([x_ref[prev, r, :, 2 * LANES:], x_ref[n, r, :, 2 * LANES:]], axis=0)
        pv.append(_mm(pi, jnp.concatenate([vw, ones], axis=1)))
    return [tuple(jnp.where(head0, t[:BAND], t[BAND:]) for t in (pvi[:, :LANES], mi, pvi[:, LANES:]))
            for pvi, mi in zip(pv, m)]


def _attn_kernel(x1_ref, x4_ref, x16_ref, out_ref, nat_s, ph_s, mid_s, s_scr):
    seq = out_ref.shape[0]
    lane = lax.broadcasted_iota(jnp.int32, (1, LANES), 1)
    head0 = lane < HEAD_DIM
    qrow = lax.broadcasted_iota(jnp.int32, (BAND, 1), 0)
    kcol = lax.broadcasted_iota(jnp.int32, (1, 2 * BAND), 1)
    rel = BAND + qrow - kcol
    bias_rest = jnp.where((rel >= 0) & (rel <= BAND), 0.0, NEG_BIG).astype(F32)
    bias_first = jnp.where(kcol < BAND, NEG_BIG, bias_rest)
    consts = (head0, bias_first, bias_rest)
    group = ATTN_GROUP
    n_groups = seq // (BAND * group)

    for bi, (d, x_ref) in enumerate(zip(DILATIONS, (x1_ref, x4_ref, x16_ref))):
        n_blocks = seq // (BAND * d)
        shift = n_blocks.bit_length() - 1
        sub_len = seq // d
        dst = nat_s.at[bi] if d == 1 else ph_s

        def probs_of(g, shift=shift, n_blocks=n_blocks):
            js = [g * group + i for i in range(group)]
            return [(j >> shift, j & (n_blocks - 1)) for j in js], js

        probs0, _ = probs_of(jnp.int32(0))
        for i, si in enumerate(_attn_scores(x_ref, probs0, head0)):
            s_scr[i] = si

        def body(g, carry, x_ref=x_ref, dst=dst, probs_of=probs_of):
            nxt, _ = probs_of(jnp.minimum(g + 1, n_groups - 1))
            s_next = _attn_scores(x_ref, nxt, head0)
            cur, js = probs_of(g)
            res = _attn_softmax_pv(x_ref, cur, [s_scr[i] for i in range(group)], consts)
            for j, aml in zip(js, res):
                rows = pl.ds(pl.multiple_of(j * BAND, BAND), BAND)
                for q, t in enumerate(aml):
                    dst[q, rows, :] = t
            for i, si in enumerate(s_next):
                s_scr[i] = si
            return carry

        lax.fori_loop(0, n_groups, body, 0)
        f = DILATIONS[1]
        if d == f:
            for r in range(d):
                staged = slice(r * sub_len, (r + 1) * sub_len)
                for q in range(3):
                    nat_s[bi, q, pl.ds(r, sub_len, stride=f), :] = ph_s[q, staged, :]
        elif d == f * f:
            for q in range(3):
                for b in range(f):
                    for a in range(f):
                        staged = slice((f * a + b) * sub_len, (f * a + b + 1) * sub_len)
                        mid_s[pl.ds(b * f * sub_len + a, sub_len, stride=f), :] = ph_s[q, staged, :]
                for b in range(f):
                    rows = slice(b * f * sub_len, (b + 1) * f * sub_len)
                    nat_s[bi, q, pl.ds(b, f * sub_len, stride=f), :] = mid_s[rows, :]

    rows_c = 512
    for c in range(seq // rows_c):
        rows = slice(c * rows_c, (c + 1) * rows_c)
        ms = [nat_s[bi, 1, rows, :] for bi in range(3)]
        top = jnp.maximum(jnp.maximum(ms[0], ms[1]), ms[2])
        es = [jnp.exp(mb - top) for mb in ms]
        num = sum(e * nat_s[bi, 0, rows, :] for bi, e in enumerate(es))
        den = sum(e * nat_s[bi, 2, rows, :] for bi, e in enumerate(es))
        out_ref[rows, :] = (num / den).astype(out_ref.dtype)


def _attention(x1, x4, x16, *, batch, seq):
    n_pairs = x1.shape[0]
    spec = lambda x: pl.BlockSpec((None, None) + x.shape[2:], lambda b, p: (p, b, 0, 0, 0, 0))
    return pl.pallas_call(
        _attn_kernel,
        grid=(batch, n_pairs),
        in_specs=[spec(x1), spec(x4), spec(x16)],
        out_specs=pl.BlockSpec((None, seq, LANES), lambda b, p: (p, b, 0)),
        out_shape=jax.ShapeDtypeStruct((n_pairs, batch * seq, LANES), BF16),
        scratch_shapes=[pltpu.VMEM((3, 3, seq, LANES), F32), pltpu.VMEM((3, seq, LANES), F32),
                        pltpu.VMEM((seq, LANES), F32),
                        pltpu.VMEM((ATTN_GROUP, 2 * BAND, 2 * BAND), F32)],
        compiler_params=pltpu.CompilerParams(
            dimension_semantics=("arbitrary", "arbitrary"), vmem_limit_bytes=VMEM_LIMIT),
        name="attention",
    )(x1, x4, x16)


def _block_diag(x, head0):
    zero = jnp.zeros_like(x)
    return jnp.concatenate([jnp.where(head0, x, zero), jnp.where(head0, zero, x)], axis=0)


def _chunk_consts():
    c = CHUNK
    trow = lax.broadcasted_iota(jnp.int32, (c, 1), 0)
    col3 = lax.broadcasted_iota(jnp.int32, (1, 3 * c), 1)
    tri3 = jnp.where((col3 & (c - 1)) <= trow, 1.0, 0.0).astype(BF16)
    lane_w = lax.broadcasted_iota(jnp.int32, (1, LANES), 1)
    scol = lane_w & (HEAD_DIM - 1)
    strict = scol < trow
    incl = scol <= trow
    eye_w = jnp.where(scol == trow, 1.0, 0.0).astype(F32)
    head0 = lane_w < HEAD_DIM
    rr = lax.broadcasted_iota(jnp.int32, (LANES, 1), 0)
    same_head = (rr // HEAD_DIM) == (lane_w // HEAD_DIM)
    levels = []
    s = 1
    while s < c:
        in_pair = (trow // (2 * s)) == (scol // (2 * s))
        levels.append(in_pair & ((trow // s) != (scol // s)) & strict)
        s *= 2
    return dict(tri3=tri3, strict=strict, incl=incl, eye_w=eye_w, head0=head0,
                same_head=same_head, levels=levels)


def _chunk_precompute(ins, consts):
    c = CHUNK
    head0 = consts["head0"]
    strict, incl, same_head = consts["strict"], consts["incl"], consts["same_head"]
    levels = consts["levels"]
    zero = jnp.zeros((c, LANES), F32)
    zero_sq = jnp.zeros((LANES, LANES), F32)

    def bd(x):
        return _block_diag(x, head0).astype(BF16)

    cums = []
    for (_, lw, _, _, _, _) in ins:
        hi = lw.astype(BF16)
        r1 = lw - hi.astype(F32)
        mid = r1.astype(BF16)
        lo = (r1 - mid.astype(F32)).astype(BF16)
        cums.append(_mm(consts["tri3"], jnp.concatenate([hi, mid, lo], axis=0)))
    yield
    ops = []
    for (r, lw, k, v, kk, b), cum in zip(ins, cums):
        cum_end = cum[c - 1:c, :]
        e_neg = jnp.exp(-cum)
        e_end = jnp.exp(cum_end - cum)
        ops.append(dict(kq=kk * jnp.exp(cum - lw), rq=r * jnp.exp(cum), kt=k * e_neg, bt=b * e_neg,
                        kh=(k * e_end).astype(BF16), bh=(b * e_end).astype(BF16),
                        g=jnp.exp(cum_end), v=v))
    a_all = [_nt(jnp.concatenate([o["kq"], o["rq"]], axis=0).astype(BF16),
                 jnp.concatenate([bd(o["bt"]), bd(o["kt"])], axis=0)) for o in ops]
    a_ab = [jnp.where(strict, a[:c, :LANES], zero) for a in a_all]
    a_rb = [jnp.where(incl, a[c:, :LANES], zero).astype(BF16) for a in a_all]
    av = [_mm(jnp.concatenate([jnp.where(strict, a[:c, LANES:], zero),
                               jnp.where(incl, a[c:, LANES:], zero)], axis=0).astype(BF16),
              bd(o["v"])) for a, o in zip(a_all, ops)]
    yield
    t_inv = [consts["eye_w"] - jnp.where(levels[0], a, zero) for a in a_ab]
    for mask in levels[1:]:
        x = [_mm(jnp.where(mask, a, zero).astype(BF16), bd(t)) for a, t in zip(a_ab, t_inv)]
        yield
        t_inv = [t - _mm(t.astype(BF16), bd(xi)) for t, xi in zip(t_inv, x)]
        yield
    twu = [_mm(t.astype(BF16), jnp.concatenate([bd(o["kq"]), bd(a[:c])], axis=1))
           for t, o, a in zip(t_inv, ops, av)]
    yield
    arb = [_mm(ar, jnp.concatenate([bd(x[:, :LANES]), bd(x[:, LANES:])], axis=1))
           for ar, x in zip(a_rb, twu)]
    yield
    out = []
    for o, a, x, y in zip(ops, av, twu, arb):
        w = x[:, :LANES].astype(BF16)
        u0 = x[:, LANES:]
        pm = -jnp.where(same_head, _tn(w, o["bh"]), zero_sq)
        dd = jnp.where(same_head,
                       _tn(jnp.concatenate([o["v"], -u0], axis=0).astype(BF16),
                           jnp.concatenate([o["kh"], o["bh"]], axis=0)), zero_sq)
        out.append((o["rq"] - y[:, :LANES], a[c:] - y[:, LANES:], pm, dd, o["g"]))
    return out


def _spread(fillers, every=2):
    out = []
    for f in fillers:
        out.append(f)
        out.extend([lambda: None] * (every - 1))
    return out


def _interleave(gen, fillers):
    fillers = list(fillers)
    while True:
        try:
            next(gen)
        except StopIteration as stop:
            result = stop.value
            break
        if fillers:
            fillers.pop(0)()
    for f in fillers:
        f()
    return result


def _chunk_apply(pre, states):
    sb = [s.astype(BF16) for s in states]
    ys = [_nt(p[0].astype(BF16), s) + p[1] for p, s in zip(pre, sb)]
    new = [st * p[4] + _mm(s, p[2].astype(BF16)) + p[3] for p, s, st in zip(pre, sb, states)]
    return ys, new


def _rwkv_kernel(*refs, first_layer, d_rwkv):
    if first_layer:
        (zr_ref, zl_ref, wl_ref, vec_ref, out_ref, vfirst_out_ref,
         r_s, lw_s, k_s, v_s, kk_s, b_s, g_s, y_s, state_s) = refs
        vfirst_ref = None
    else:
        (zr_ref, zl_ref, vfirst_ref, wl_ref, vec_ref, out_ref,
         r_s, lw_s, k_s, v_s, kk_s, b_s, g_s, y_s, state_s) = refs
    tc = zr_ref.shape[0]
    n_pairs = d_rwkv // LANES
    c = CHUNK
    sub = RWKV_SUB
    sub_chunks = sub // c

    @pl.when(pl.program_id(1) == 0)
    def _():
        state_s[...] = jnp.zeros_like(state_s)

    w0, a0, mv0, k_k, k_a, r_k, gn_w, gn_b = (vec_ref[j:j + 1, :] for j in range(8))
    hrow = lax.broadcasted_iota(jnp.int32, (2 * LANES, 1), 0)
    hcol = lax.broadcasted_iota(jnp.int32, (1, 2 * LANES), 1)
    same = jnp.where((hrow // HEAD_DIM) == (hcol // HEAD_DIM), 1.0, 0.0).astype(BF16)

    def head_sum(x):
        xb = x.astype(BF16)
        w = 2 * LANES
        return jnp.concatenate([_mm(xb[:, j * w:(j + 1) * w], same) for j in range(d_rwkv // w)],
                               axis=1)

    lane = lax.broadcasted_iota(jnp.int32, (1, LORA_PAD), 1)
    o1 = LORA_DECAY
    o2 = o1 + LORA_AAA
    o3 = o2 + LORA_GATE

    def prepare(rows):
        zl = zl_ref[rows, :]
        act = jnp.where(lane < o1, jnp.tanh(zl),
                        jnp.where((lane >= o2) & (lane < o3), jax.nn.sigmoid(zl), zl))
        lora = _mm(act.astype(BF16), wl_ref[...])
        r = zr_ref[rows, 0:d_rwkv]
        k = zr_ref[rows, d_rwkv:2 * d_rwkv]
        v = zr_ref[rows, 2 * d_rwkv:3 * d_rwkv]
        lw = -math.exp(-0.5) * jax.nn.sigmoid(w0 + lora[:, 0:d_rwkv])
        a = jax.nn.sigmoid(a0 + lora[:, d_rwkv:2 * d_rwkv])
        if first_layer:
            vfirst_out_ref[rows, :] = v
        else:
            vgate = jax.nn.sigmoid(mv0 + lora[:, 3 * d_rwkv:4 * d_rwkv])
            v = v + (vfirst_ref[rows, :] - v) * vgate
        kk = k * k_k
        kk = kk / jnp.maximum(jnp.sqrt(head_sum(kk * kk)), 1e-12)
        r_s[rows, :] = r
        lw_s[rows, :] = lw
        k_s[rows, :] = k * (1.0 + (a - 1.0) * k_a)
        v_s[rows, :] = v
        kk_s[rows, :] = kk
        b_s[rows, :] = kk * a
        g_s[rows, :] = lora[:, 2 * d_rwkv:3 * d_rwkv]

    def problems(h):
        ins = []
        for ci in range(h * sub_chunks, (h + 1) * sub_chunks):
            rows = slice(ci * c, (ci + 1) * c)
            for p in range(n_pairs):
                ls = slice(p * LANES, (p + 1) * LANES)
                ins.append(tuple(ref[rows, ls] for ref in (r_s, lw_s, k_s, v_s, kk_s, b_s)))
        return ins

    states = [state_s[p] for p in range(n_pairs)]

    def chain_steps(h, pre):
        def step(j):
            def run():
                ys, new = _chunk_apply(pre[j * n_pairs:(j + 1) * n_pairs], states)
                states[:] = new
                ci = h * sub_chunks + j
                for p in range(n_pairs):
                    y_s[ci * c:(ci + 1) * c, p * LANES:(p + 1) * LANES] = ys[p]
            return run

        rows = slice(h * sub, (h + 1) * sub)
        tmp = {}

        def stats():
            y = y_s[rows, :]
            tmp["dlt"] = y - head_sum(y) * (1.0 / HEAD_DIM)
            tmp["bonus"] = head_sum(r_s[rows, :] * k_s[rows, :] * r_k) * v_s[rows, :]

        def finish():
            dlt = tmp["dlt"]
            var = head_sum(dlt * dlt) * (1.0 / HEAD_DIM)
            yn = dlt * lax.rsqrt(var + GN_EPS) * gn_w + gn_b
            out_ref[rows, :] = ((yn + tmp["bonus"]) * g_s[rows, :]).astype(out_ref.dtype)

        return [step(j) for j in range(sub_chunks)] + [stats, finish]

    consts = _chunk_consts()
    n_sub = tc // sub
    for h in range(n_sub):
        prepare(slice(h * sub, (h + 1) * sub))
    pending = []
    for h in range(n_sub):
        pre = _interleave(_chunk_precompute(problems(h), consts), _spread(pending))
        pending = chain_steps(h, pre)
    for f in pending:
        f()
    for p in range(n_pairs):
        state_s[p] = states[p]


def _rwkv(zr, zl, vfirst, wl, vecs, *, batch, seq, tc=RWKV_TC):
    t = zr.shape[0]
    d_rwkv = zr.shape[1] // 3
    first_layer = vfirst is None
    tiles = seq // tc
    row = lambda b, i: (b * tiles + i, 0)
    const = lambda b, i: (0, 0)
    in_specs = [pl.BlockSpec((tc, 3 * d_rwkv), row), pl.BlockSpec((tc, LORA_PAD), row)]
    args = [zr, zl]
    if not first_layer:
        in_specs.append(pl.BlockSpec((tc, d_rwkv), row))
        args.append(vfirst)
    in_specs += [pl.BlockSpec(wl.shape, const), pl.BlockSpec(vecs.shape, const)]
    args += [wl, vecs]
    out_specs = [pl.BlockSpec((tc, d_rwkv), row)]
    out_shape = [jax.ShapeDtypeStruct((t, d_rwkv), BF16)]
    if first_layer:
        out_specs.append(pl.BlockSpec((tc, d_rwkv), row))
        out_shape.append(jax.ShapeDtypeStruct((t, d_rwkv), F32))
    kern = functools.partial(_rwkv_kernel, first_layer=first_layer, d_rwkv=d_rwkv)
    res = pl.pallas_call(
        kern,
        grid=(batch, tiles),
        in_specs=in_specs,
        out_specs=out_specs,
        out_shape=out_shape,
        scratch_shapes=[pltpu.VMEM((tc, d_rwkv), F32) for _ in range(8)]
        + [pltpu.VMEM((d_rwkv // LANES, LANES, LANES), F32)],
        compiler_params=pltpu.CompilerParams(
            dimension_semantics=("arbitrary", "arbitrary"), vmem_limit_bytes=VMEM_LIMIT),
        name="rwkv7_first" if first_layer else "rwkv7",
    )(*args)
    return (res[0], res[1]) if first_layer else (res[0], vfirst)


def _mix_ffn_kernel(a_ref, rw_ref, x_ref, gains_ref, ga_ref, wo_ref, wu_ref, wd_ref, out_ref,
                    acc_ref, *, ff_chunk):
    n_pairs = a_ref.shape[0]
    g_post, g_pre, g_ffn = (gains_ref[j:j + 1, :] for j in range(3))
    attn = jnp.concatenate([a_ref[p].astype(F32) for p in range(n_pairs)], axis=-1)
    attn = _rms(attn, ga_ref[...]).astype(BF16)
    d_attn = attn.shape[1]
    mixed = _mm(attn, wo_ref[0:d_attn, :]) + _mm(rw_ref[...], wo_ref[d_attn:, :])
    x1 = x_ref[...] + _rms(mixed, g_post)
    h = _rms(x1, g_pre).astype(BF16)
    d_ff = wu_ref.shape[1]
    for c in range(d_ff // ff_chunk):
        cs = slice(c * ff_chunk, (c + 1) * ff_chunk)
        u = jnp.maximum(_mm(h, wu_ref[:, cs]), 0.0)
        part = _mm((u * u).astype(BF16), wd_ref[cs, :])
        if c == 0:
            acc_ref[...] = part
        else:
            acc_ref[...] += part
    out_ref[...] = x1 + _rms(acc_ref[...], g_ffn)


def _mix_ffn(attn, rw, x2, gains, ga, wo, wu, wd, *, tm=512, ff_chunk=1024):
    t, d = x2.shape
    n_pairs = attn.shape[0]
    d_rwkv = rw.shape[1]
    row = lambda i: (i, 0)
    const = lambda i: (0, 0)
    resident = lambda shape: pl.BlockSpec(shape, const, pipeline_mode=pl.Buffered(1))
    return pl.pallas_call(
        functools.partial(_mix_ffn_kernel, ff_chunk=ff_chunk),
        grid=(t // tm,),
        in_specs=[pl.BlockSpec((n_pairs, tm, LANES), lambda i: (0, i, 0)),
                  pl.BlockSpec((tm, d_rwkv), row), pl.BlockSpec((tm, d), row),
                  pl.BlockSpec(gains.shape, const), pl.BlockSpec(ga.shape, const),
                  resident(wo.shape), resident(wu.shape), resident(wd.shape)],
        out_specs=pl.BlockSpec((tm, d), row),
        out_shape=jax.ShapeDtypeStruct((t, d), F32),
        scratch_shapes=[pltpu.VMEM((tm, d), F32)],
        compiler_params=pltpu.CompilerParams(
            dimension_semantics=("arbitrary",), vmem_limit_bytes=VMEM_LIMIT),
        name="mix_ffn",
    )(attn, rw, x2, gains, ga, wo, wu, wd)


def _layer_params(i, d_attn, d_rwkv, w_in_first, w_in_rest, mu_shift, mu_shift_mv,
                  decay_up, aaa_up, gate_up, mv_up):
    w = w_in_first if i == 0 else w_in_rest[i - 1]
    d = w.shape[0]
    n_first = w_in_first.shape[1]
    lora_used = LORA_DECAY + LORA_AAA + LORA_GATE + LORA_MV
    n_pairs = d_attn // LANES
    qkv = [w[:, j * d_attn + p * LANES:j * d_attn + (p + 1) * LANES]
           for p in range(n_pairs) for j in range(3)]
    pieces = qkv + [w[:, 3 * d_attn:n_first]]
    mus = [mu_shift[i]]
    if i == 0:
        pieces.append(jnp.zeros((d, LORA_MV), F32))
        mus.append(jnp.zeros((LORA_MV,), F32))
    else:
        pieces.append(w[:, n_first:])
        mus.append(mu_shift_mv[i - 1])
    pieces.append(jnp.zeros((d, LORA_PAD - lora_used), F32))
    mus.append(jnp.zeros((LORA_PAD - lora_used,), F32))
    w_r = jnp.concatenate(pieces, axis=1).astype(BF16)
    mu_r = jnp.concatenate(mus)[None, :]
    wl = jnp.zeros((LORA_PAD, 4 * d_rwkv), F32)
    o1 = LORA_DECAY
    o2 = o1 + LORA_AAA
    o3 = o2 + LORA_GATE
    wl = wl.at[0:o1, 0:d_rwkv].set(decay_up[i])
    wl = wl.at[o1:o2, d_rwkv:2 * d_rwkv].set(aaa_up[i])
    wl = wl.at[o2:o3, 2 * d_rwkv:3 * d_rwkv].set(gate_up[i])
    if i > 0:
        wl = wl.at[o3:o3 + LORA_MV, 3 * d_rwkv:4 * d_rwkv].set(mv_up[i - 1])
    return w_r, mu_r, wl.astype(BF16)


def kernel(x, norm_mix_pre, norm_mix_post, norm_ffn_pre, norm_ffn_post, w_in_first, w_in_rest,
           mu_shift, mu_shift_mv, attn_out_gain, decay_w0, decay_up, aaa_a0, aaa_up, mv_v0, mv_up,
           gate_up, k_k, k_a, r_k, gn_w, gn_b, w_out, w_ffn_up, w_ffn_down):
    batch, seq, d = x.shape
    depth = norm_mix_pre.shape[0]
    d_attn = attn_out_gain.shape[1]
    d_rwkv = decay_w0.shape[1]
    x2 = x.reshape(batch * seq, d)
    vfirst = None
    for i in range(depth):
        w_r, mu_r, wl = _layer_params(i, d_attn, d_rwkv, w_in_first, w_in_rest, mu_shift,
                                      mu_shift_mv, decay_up, aaa_up, gate_up, mv_up)
        mv0 = mv_v0[i - 1] if i > 0 else jnp.zeros((d_rwkv,), F32)
        vecs = jnp.stack([decay_w0[i], aaa_a0[i], mv0, k_k[i], k_a[i], r_k[i].reshape(-1),
                          gn_w[i], gn_b[i]])
        x1, x4, x16, zr, zl = _inproj(x2, norm_mix_pre[i][None, :], w_r, mu_r, batch=batch, seq=seq)
        attn = _attention(x1, x4, x16, batch=batch, seq=seq)
        rw, vfirst = _rwkv(zr, zl, vfirst, wl, vecs, batch=batch, seq=seq)
        gains = jnp.stack([norm_mix_post[i], norm_ffn_pre[i], norm_ffn_post[i]])
        x2 = _mix_ffn(attn, rw, x2, gains, attn_out_gain[i][None, :], w_out[i].astype(BF16),
                      w_ffn_up[i].astype(BF16), w_ffn_down[i].astype(BF16))
    return x2.reshape(batch, seq, d)
```

```python
import functools
import math

import jax
import jax.numpy as jnp
from jax import lax
from jax.experimental import pallas as pl
from jax.experimental.pallas import tpu as pltpu

F32 = jnp.float32
BF16 = jnp.bfloat16

HEAD_DIM = 64
LANES = 128
NORM_EPS = 1e-6
GN_EPS = 64e-5
DILATIONS = (1, 4, 16)
BAND = 128
QKV_W = 3 * LANES
INPROJ_TM = BAND * DILATIONS[1]
ATTN_GROUP = 4
LORA_DECAY, LORA_AAA, LORA_GATE, LORA_MV = 32, 32, 96, 32
LORA_PAD = 256
CHUNK = 64
RWKV_TC = 512
RWKV_SUB = 256
NEG_BIG = -1e30
VMEM_LIMIT = 56 * 1024 * 1024


def _nt(a, b):
    return lax.dot_general(a, b, (((1,), (1,)), ((), ())), preferred_element_type=F32)


def _tn(a, b):
    return lax.dot_general(a, b, (((0,), (0,)), ((), ())), preferred_element_type=F32)


def _mm(a, b):
    return jnp.dot(a, b, preferred_element_type=F32)


def _rms(x, g):
    return x * lax.rsqrt(jnp.mean(x * x, axis=-1, keepdims=True) + NORM_EPS) * g


def _inproj_kernel(x_ref, g_ref, w_ref, mu_ref, x1_ref, x4_ref, x16_ref, zr_ref, zl_ref,
                   carry_ref, zs_ref, z4_ref, *, seq_tiles, n_pairs, d_rwkv3):
    i = pl.program_id(0)
    tm = x_ref.shape[0]
    h = _rms(x_ref[...], g_ref[...]).astype(BF16)
    for p in range(n_pairs):
        if p % 2 == 0:
            zz = _mm(h, w_ref[:, p * QKV_W:(p + 2) * QKV_W])
        z = zz[:, (p % 2) * QKV_W:(p % 2 + 1) * QKV_W]
        x1_ref[p, :, 0] = z.astype(BF16).reshape(tm // BAND, BAND, QKV_W)
        f = DILATIONS[1]
        for j in range(QKV_W // LANES):
            ls = slice(j * LANES, (j + 1) * LANES)
            zs_ref[p, j] = z[:, ls]
            for b in range(f):
                z4 = zs_ref[p, j, pl.ds(b, tm // f, stride=f), :]
                x4_ref[p, b, :, ls] = z4.astype(BF16)
                z4_ref[p, j, b] = z4
                for a in range(f):
                    z16 = z4_ref[p, j, b, pl.ds(a, tm // (f * f), stride=f), :]
                    x16_ref[p, f * a + b, :, ls] = z16.astype(BF16)
    d_attn3 = n_pairs * QKV_W
    first_tile = (i % seq_tiles) == 0
    row0 = lax.broadcasted_iota(jnp.int32, (tm, 1), 0) == 0
    n_shift = d_rwkv3 + LORA_PAD
    cw = 512
    c0 = 0
    while c0 < n_shift:
        w = min(cw, n_shift - c0)
        z = _mm(h, w_ref[:, d_attn3 + c0:d_attn3 + c0 + w])
        carry = jnp.where(first_tile, 0.0, carry_ref[:, c0:c0 + w])
        prev = jnp.where(row0, carry, pltpu.roll(z, 1, 0))
        carry_ref[:, c0:c0 + w] = z[tm - 1:tm, :]
        zs = z + (prev - z) * mu_ref[:, c0:c0 + w]
        if c0 < d_rwkv3:
            zr_ref[:, c0:c0 + w] = zs
        else:
            zl_ref[...] = zs
        c0 += w


def _inproj(x2, g, w, mu, *, batch, seq):
    tm = INPROJ_TM
    t, d = x2.shape
    nc = w.shape[1]
    n_pairs = (d // 2) // LANES
    d_attn3 = n_pairs * QKV_W
    d_rwkv3 = nc - d_attn3 - LORA_PAD
    seq_tiles = seq // tm
    d4, d16 = DILATIONS[1], DILATIONS[2]
    assert tm == BAND * d4 and d16 == d4 * d4 and seq % (BAND * d16) == 0
    sub16 = BAND * d16 // tm
    kern = functools.partial(_inproj_kernel, seq_tiles=seq_tiles, n_pairs=n_pairs, d_rwkv3=d_rwkv3)
    return pl.pallas_call(
        kern,
        grid=(t // tm,),
        in_specs=[
            pl.BlockSpec((tm, d), lambda i: (i, 0)),
            pl.BlockSpec((1, d), lambda i: (0, 0)),
            pl.BlockSpec((d, nc), lambda i: (0, 0)),
            pl.BlockSpec((1, nc - d_attn3), lambda i: (0, 0)),
        ],
        out_specs=[
            pl.BlockSpec((n_pairs, None, tm // BAND, 1, BAND, QKV_W),
                         lambda i: (0, i // seq_tiles, i % seq_tiles, 0, 0, 0)),
            pl.BlockSpec((n_pairs, None, None, d4, BAND, QKV_W),
                         lambda i: (0, i // seq_tiles, i % seq_tiles, 0, 0, 0)),
            pl.BlockSpec((n_pairs, None, None, d16, tm // d16, QKV_W),
                         lambda i: (0, i // seq_tiles, (i % seq_tiles) // sub16, 0,
                                    (i % seq_tiles) % sub16, 0)),
            pl.BlockSpec((tm, d_rwkv3), lambda i: (i, 0)),
            pl.BlockSpec((tm, LORA_PAD), lambda i: (i, 0)),
        ],
        out_shape=[
            jax.ShapeDtypeStruct((n_pairs, batch, seq // BAND, 1, BAND, QKV_W), BF16),
            jax.ShapeDtypeStruct((n_pairs, batch, seq // (BAND * d4), d4, BAND, QKV_W), BF16),
            jax.ShapeDtypeStruct((n_pairs, batch, seq // (BAND * d16), d16, BAND, QKV_W), BF16),
            jax.ShapeDtypeStruct((t, d_rwkv3), F32),
            jax.ShapeDtypeStruct((t, LORA_PAD), F32),
        ],
        scratch_shapes=[pltpu.VMEM((1, nc - d_attn3), F32),
                        pltpu.VMEM((n_pairs, QKV_W // LANES, tm, LANES), F32),
                        pltpu.VMEM((n_pairs, QKV_W // LANES, d4, tm // d4, LANES), F32)],
        compiler_params=pltpu.CompilerParams(
            dimension_semantics=("arbitrary",), vmem_limit_bytes=VMEM_LIMIT),
        name="inproj",
    )(x2, g, w, mu)


def _attn_scores(x_ref, probs, head0):
    scale = 1.0 / math.sqrt(HEAD_DIM)
    out = []
    for r, n in probs:
        prev = jnp.maximum(n - 1, 0)
        q = x_ref[n, r, :, 0:LANES] * scale
        kw = jnp.concatenate([x_ref[prev, r, :, LANES:2 * LANES], x_ref[n, r, :, LANES:2 * LANES]],
                             axis=0)
        out.append(_nt(_block_diag(q, head0), kw))
    return out


def _attn_softmax(probs, scores, consts):
    head0, bias_first, bias_rest = consts
    s = []
    for (r, n), si in zip(probs, scores):
        bias = jnp.where(n == 0, bias_first, bias_rest)
        s.append(jnp.concatenate([si[:BAND] + bias, si[BAND:] + bias], axis=0))
    m = [jnp.max(si, axis=-1, keepdims=True) for si in s]
    p = [jnp.exp(si - mi).astype(BF16) for si, mi in zip(s, m)]
    return p, [jnp.where(head0, mi[:BAND], mi[BAND:]) for mi in m]


def _attn_pv(x_ref, probs, p, head0):
    ones = jnp.ones((2 * BAND, LANES), BF16)
    pv = []
    for (r, n), pi in zip(probs, p):
        prev = jnp.maximum(n - 1, 0)
        vw = jnp.concatenate([x_ref[prev, r, :, 2 * LANES:], x_ref[n, r, :, 2 * LANES:]], axis=0)
        pv.append(_mm(pi, jnp.concatenate([vw, ones], axis=1)))
    return [tuple(jnp.where(head0, t[:BAND], t[BAND:]) for t in (pvi[:, :LANES], pvi[:, LANES:]))
            for pvi in pv]


def _attn_kernel(x1_ref, x4_ref, x16_ref, out_ref, nat_s, ph_s, mid_s, p_scr):
    seq = out_ref.shape[0]
    lane = lax.broadcasted_iota(jnp.int32, (1, LANES), 1)
    head0 = lane < HEAD_DIM
    qrow = lax.broadcasted_iota(jnp.int32, (BAND, 1), 0)
    kcol = lax.broadcasted_iota(jnp.int32, (1, 2 * BAND), 1)
    rel = BAND + qrow - kcol
    bias_rest = jnp.where((rel >= 0) & (rel <= BAND), 0.0, NEG_BIG).astype(F32)
    bias_first = jnp.where(kcol < BAND, NEG_BIG, bias_rest)
    consts = (head0, bias_first, bias_rest)
    group = ATTN_GROUP
    n_groups = seq // (BAND * group)

    for bi, (d, x_ref) in reversed(list(enumerate(zip(DILATIONS, (x1_ref, x4_ref, x16_ref))))):
        n_blocks = seq // (BAND * d)
        shift = n_blocks.bit_length() - 1
        sub_len = seq // d
        dst = nat_s.at[bi] if d == 1 else ph_s

        def probs_of(g, shift=shift, n_blocks=n_blocks):
            js = [g * group + i for i in range(group)]
            return [(j >> shift, j & (n_blocks - 1)) for j in js], js

        def store(js, q, vals, dst=dst):
            for j, t in zip(js, vals):
                dst[q, pl.ds(pl.multiple_of(j * BAND, BAND), BAND), :] = t

        probs0, js0 = probs_of(jnp.int32(0))
        p0, m0 = _attn_softmax(probs0, _attn_scores(x_ref, probs0, head0), consts)
        store(js0, 1, m0)
        for i, pi in enumerate(p0):
            p_scr[i] = pi

        def body(g, carry, x_ref=x_ref, probs_of=probs_of, store=store):
            nxt, js_next = probs_of(jnp.minimum(g + 1, n_groups - 1))
            s_next = _attn_scores(x_ref, nxt, head0)
            cur, js = probs_of(g)
            acc_l = _attn_pv(x_ref, cur, [p_scr[i] for i in range(group)], head0)
            p_next, m_next = _attn_softmax(nxt, s_next, consts)
            store(js, 0, [t[0] for t in acc_l])
            store(js, 2, [t[1] for t in acc_l])
            store(js_next, 1, m_next)
            for i, pi in enumerate(p_next):
                p_scr[i] = pi
            return carry

        lax.fori_loop(0, n_groups, body, 0, unroll=True)
        f = DILATIONS[1]
        if d == f:
            for r in range(d):
                staged = slice(r * sub_len, (r + 1) * sub_len)
                for q in range(3):
                    nat_s[bi, q, pl.ds(r, sub_len, stride=f), :] = ph_s[q, staged, :]
        elif d == f * f:
            for q in range(3):
                for b in range(f):
                    for a in range(f):
                        staged = slice((f * a + b) * sub_len, (f * a + b + 1) * sub_len)
                        mid_s[pl.ds(b * f * sub_len + a, sub_len, stride=f), :] = ph_s[q, staged, :]
                for b in range(f):
                    rows = slice(b * f * sub_len, (b + 1) * f * sub_len)
                    nat_s[bi, q, pl.ds(b, f * sub_len, stride=f), :] = mid_s[rows, :]

    rows_c = 512
    for c in range(seq // rows_c):
        rows = slice(c * rows_c, (c + 1) * rows_c)
        ms = [nat_s[bi, 1, rows, :] for bi in range(3)]
        top = jnp.maximum(jnp.maximum(ms[0], ms[1]), ms[2])
        es = [jnp.exp(mb - top) for mb in ms]
        num = sum(e * nat_s[bi, 0, rows, :] for bi, e in enumerate(es))
        den = sum(e * nat_s[bi, 2, rows, :] for bi, e in enumerate(es))
        out_ref[rows, :] = (num / den).astype(out_ref.dtype)


def _attention(x1, x4, x16, *, batch, seq):
    n_pairs = x1.shape[0]
    spec = lambda x: pl.BlockSpec((None, None) + x.shape[2:], lambda b, p: (p, b, 0, 0, 0, 0))
    return pl.pallas_call(
        _attn_kernel,
        grid=(batch, n_pairs),
        in_specs=[spec(x1), spec(x4), spec(x16)],
        out_specs=pl.BlockSpec((None, seq, LANES), lambda b, p: (p, b, 0)),
        out_shape=jax.ShapeDtypeStruct((n_pairs, batch * seq, LANES), BF16),
        scratch_shapes=[pltpu.VMEM((3, 3, seq, LANES), F32), pltpu.VMEM((3, seq, LANES), F32),
                        pltpu.VMEM((seq, LANES), F32),
                        pltpu.VMEM((ATTN_GROUP, 2 * BAND, 2 * BAND), BF16)],
        compiler_params=pltpu.CompilerParams(
            dimension_semantics=("arbitrary", "arbitrary"), vmem_limit_bytes=VMEM_LIMIT),
        name="attention",
    )(x1, x4, x16)


def _block_diag(x, head0):
    zero = jnp.zeros_like(x)
    return jnp.concatenate([jnp.where(head0, x, zero), jnp.where(head0, zero, x)], axis=0)


def _chunk_consts():
    c = CHUNK
    trow = lax.broadcasted_iota(jnp.int32, (c, 1), 0)
    col3 = lax.broadcasted_iota(jnp.int32, (1, 3 * c), 1)
    tri3 = jnp.where((col3 & (c - 1)) <= trow, 1.0, 0.0).astype(BF16)
    lane_w = lax.broadcasted_iota(jnp.int32, (1, LANES), 1)
    scol = lane_w & (HEAD_DIM - 1)
    strict = scol < trow
    incl = scol <= trow
    eye_w = jnp.where(scol == trow, 1.0, 0.0).astype(F32)
    head0 = lane_w < HEAD_DIM
    rr = lax.broadcasted_iota(jnp.int32, (LANES, 1), 0)
    same_head = (rr // HEAD_DIM) == (lane_w // HEAD_DIM)
    levels = []
    s = 1
    while s < c:
        in_pair = (trow // (2 * s)) == (scol // (2 * s))
        levels.append(in_pair & ((trow // s) != (scol // s)) & strict)
        s *= 2
    return dict(tri3=tri3, strict=strict, incl=incl, eye_w=eye_w, head0=head0,
                same_head=same_head, levels=levels)


def _chunk_precompute(ins, consts):
    c = CHUNK
    head0 = consts["head0"]
    strict, incl, same_head = consts["strict"], consts["incl"], consts["same_head"]
    levels = consts["levels"]
    zero = jnp.zeros((c, LANES), F32)
    zero_sq = jnp.zeros((LANES, LANES), F32)

    def bd(x):
        return _block_diag(x, head0).astype(BF16)

    cums = []
    for (_, lw, _, _, _, _) in ins:
        hi = lw.astype(BF16)
        r1 = lw - hi.astype(F32)
        mid = r1.astype(BF16)
        lo = (r1 - mid.astype(F32)).astype(BF16)
        cums.append(_mm(consts["tri3"], jnp.concatenate([hi, mid, lo], axis=0)))
    yield
    ops = []
    for (r, lw, k, v, kk, b), cum in zip(ins, cums):
        cum_end = cum[c - 1:c, :]
        e_neg = jnp.exp(-cum)
        e_end = jnp.exp(cum_end - cum)
        ops.append(dict(kq=kk * jnp.exp(cum - lw), rq=r * jnp.exp(cum), kt=k * e_neg, bt=b * e_neg,
                        kh=(k * e_end).astype(BF16), bh=(b * e_end).astype(BF16),
                        g=jnp.exp(cum_end), v=v))
    a_all = [_nt(jnp.concatenate([o["kq"], o["rq"]], axis=0).astype(BF16),
                 jnp.concatenate([bd(o["bt"]), bd(o["kt"])], axis=0)) for o in ops]
    a_ab = [jnp.where(strict, a[:c, :LANES], zero) for a in a_all]
    a_rb = [jnp.where(incl, a[c:, :LANES], zero).astype(BF16) for a in a_all]
    av = [_mm(jnp.concatenate([jnp.where(strict, a[:c, LANES:], zero),
                               jnp.where(incl, a[c:, LANES:], zero)], axis=0).astype(BF16),
              bd(o["v"])) for a, o in zip(a_all, ops)]
    yield
    t_inv = [consts["eye_w"] - jnp.where(levels[0], a, zero) for a in a_ab]
    for mask in levels[1:]:
        x = [_mm(jnp.where(mask, a, zero).astype(BF16), bd(t)) for a, t in zip(a_ab, t_inv)]
        yield
        t_inv = [t - _mm(t.astype(BF16), bd(xi)) for t, xi in zip(t_inv, x)]
        yield
    twu = [_mm(t.astype(BF16), jnp.concatenate([bd(o["kq"]), bd(a[:c])], axis=1))
           for t, o, a in zip(t_inv, ops, av)]
    yield
    arb = [_mm(ar, jnp.concatenate([bd(x[:, :LANES]), bd(x[:, LANES:])], axis=1))
           for ar, x in zip(a_rb, twu)]
    yield
    out = []
    for o, a, x, y in zip(ops, av, twu, arb):
        w = x[:, :LANES].astype(BF16)
        u0 = x[:, LANES:]
        pm = -jnp.where(same_head, _tn(w, o["bh"]), zero_sq)
        dd = jnp.where(same_head,
                       _tn(jnp.concatenate([o["v"], -u0], axis=0).astype(BF16),
                           jnp.concatenate([o["kh"], o["bh"]], axis=0)), zero_sq)
        out.append((o["rq"] - y[:, :LANES], a[c:] - y[:, LANES:], pm, dd, o["g"]))
    return out


def _spread(fillers, every=2):
    out = []
    for f in fillers:
        out.append(f)
        out.extend([lambda: None] * (every - 1))
    return out


def _interleave(gen, fillers):
    fillers = list(fillers)
    while True:
        try:
            next(gen)
        except StopIteration as stop:
            result = stop.value
            break
        if fillers:
            fillers.pop(0)()
    for f in fillers:
        f()
    return result


def _chunk_apply(pre, states):
    sb = [s.astype(BF16) for s in states]
    ys = [_nt(p[0].astype(BF16), s) + p[1] for p, s in zip(pre, sb)]
    new = [st * p[4] + _mm(s, p[2].astype(BF16)) + p[3] for p, s, st in zip(pre, sb, states)]
    return ys, new


def _rwkv_kernel(*refs, first_layer, d_rwkv):
    if first_layer:
        (zr_ref, zl_ref, wl_ref, vec_ref, out_ref, vfirst_out_ref,
         r_s, lw_s, k_s, v_s, kk_s, b_s, g_s, y_s, state_s) = refs
        vfirst_ref = None
    else:
        (zr_ref, zl_ref, vfirst_ref, wl_ref, vec_ref, out_ref,
         r_s, lw_s, k_s, v_s, kk_s, b_s, g_s, y_s, state_s) = refs
    tc = zr_ref.shape[0]
    n_pairs = d_rwkv // LANES
    c = CHUNK
    sub = RWKV_SUB
    sub_chunks = sub // c

    @pl.when(pl.program_id(1) == 0)
    def _():
        state_s[...] = jnp.zeros_like(state_s)

    w0, a0, mv0, k_k, k_a, r_k, gn_w, gn_b = (vec_ref[j:j + 1, :] for j in range(8))
    hrow = lax.broadcasted_iota(jnp.int32, (2 * LANES, 1), 0)
    hcol = lax.broadcasted_iota(jnp.int32, (1, 2 * LANES), 1)
    same = jnp.where((hrow // HEAD_DIM) == (hcol // HEAD_DIM), 1.0, 0.0).astype(BF16)

    def head_sum(x):
        xb = x.astype(BF16)
        w = 2 * LANES
        return jnp.concatenate([_mm(xb[:, j * w:(j + 1) * w], same) for j in range(d_rwkv // w)],
                               axis=1)

    lane = lax.broadcasted_iota(jnp.int32, (1, LORA_PAD), 1)
    o1 = LORA_DECAY
    o2 = o1 + LORA_AAA
    o3 = o2 + LORA_GATE

    def prepare(rows):
        zl = zl_ref[rows, :]
        act = jnp.where(lane < o1, jnp.tanh(zl),
                        jnp.where((lane >= o2) & (lane < o3), jax.nn.sigmoid(zl), zl))
        lora = _mm(act.astype(BF16), wl_ref[...])
        r = zr_ref[rows, 0:d_rwkv]
        k = zr_ref[rows, d_rwkv:2 * d_rwkv]
        v = zr_ref[rows, 2 * d_rwkv:3 * d_rwkv]
        lw = -math.exp(-0.5) * jax.nn.sigmoid(w0 + lora[:, 0:d_rwkv])
        a = jax.nn.sigmoid(a0 + lora[:, d_rwkv:2 * d_rwkv])
        if first_layer:
            vfirst_out_ref[rows, :] = v
        else:
            vgate = jax.nn.sigmoid(mv0 + lora[:, 3 * d_rwkv:4 * d_rwkv])
            v = v + (vfirst_ref[rows, :] - v) * vgate
        kk = k * k_k
        kk = kk / jnp.maximum(jnp.sqrt(head_sum(kk * kk)), 1e-12)
        r_s[rows, :] = r
        lw_s[rows, :] = lw
        k_s[rows, :] = k * (1.0 + (a - 1.0) * k_a)
        v_s[rows, :] = v
        kk_s[rows, :] = kk
        b_s[rows, :] = kk * a
        g_s[rows, :] = lora[:, 2 * d_rwkv:3 * d_rwkv]

    def problems(h):
        ins = []
        for ci in range(h * sub_chunks, (h + 1) * sub_chunks):
            rows = slice(ci * c, (ci + 1) * c)
            for p in range(n_pairs):
                ls = slice(p * LANES, (p + 1) * LANES)
                ins.append(tuple(ref[rows, ls] for ref in (r_s, lw_s, k_s, v_s, kk_s, b_s)))
        return ins

    states = [state_s[p] for p in range(n_pairs)]

    def chain_steps(h, pre):
        def step(j):
            def run():
                ys, new = _chunk_apply(pre[j * n_pairs:(j + 1) * n_pairs], states)
                states[:] = new
                ci = h * sub_chunks + j
                for p in range(n_pairs):
                    y_s[ci * c:(ci + 1) * c, p * LANES:(p + 1) * LANES] = ys[p]
            return run

        rows = slice(h * sub, (h + 1) * sub)
        tmp = {}

        def stats():
            y = y_s[rows, :]
            tmp["dlt"] = y - head_sum(y) * (1.0 / HEAD_DIM)
            tmp["bonus"] = head_sum(r_s[rows, :] * k_s[rows, :] * r_k) * v_s[rows, :]

        def finish():
            dlt = tmp["dlt"]
            var = head_sum(dlt * dlt) * (1.0 / HEAD_DIM)
            yn = dlt * lax.rsqrt(var + GN_EPS) * gn_w + gn_b
            out_ref[rows, :] = ((yn + tmp["bonus"]) * g_s[rows, :]).astype(out_ref.dtype)

        return [step(j) for j in range(sub_chunks)] + [stats, finish]

    consts = _chunk_consts()
    n_sub = tc // sub
    for h in range(n_sub):
        prepare(slice(h * sub, (h + 1) * sub))
    pending = []
    for h in range(n_sub):
        pre = _interleave(_chunk_precompute(problems(h), consts), _spread(pending))
        pending = chain_steps(h, pre)
    for f in pending:
        f()
    for p in range(n_pairs):
        state_s[p] = states[p]


def _rwkv(zr, zl, vfirst, wl, vecs, *, batch, seq, tc=RWKV_TC):
    t = zr.shape[0]
    d_rwkv = zr.shape[1] // 3
    first_layer = vfirst is None
    tiles = seq // tc
    row = lambda b, i: (b * tiles + i, 0)
    const = lambda b, i: (0, 0)
    in_specs = [pl.BlockSpec((tc, 3 * d_rwkv), row), pl.BlockSpec((tc, LORA_PAD), row)]
    args = [zr, zl]
    if not first_layer:
        in_specs.append(pl.BlockSpec((tc, d_rwkv), row))
        args.append(vfirst)
    in_specs += [pl.BlockSpec(wl.shape, const), pl.BlockSpec(vecs.shape, const)]
    args += [wl, vecs]
    out_specs = [pl.BlockSpec((tc, d_rwkv), row)]
    out_shape = [jax.ShapeDtypeStruct((t, d_rwkv), BF16)]
    if first_layer:
        out_specs.append(pl.BlockSpec((tc, d_rwkv), row))
        out_shape.append(jax.ShapeDtypeStruct((t, d_rwkv), F32))
    kern = functools.partial(_rwkv_kernel, first_layer=first_layer, d_rwkv=d_rwkv)
    res = pl.pallas_call(
        kern,
        grid=(batch, tiles),
        in_specs=in_specs,
        out_specs=out_specs,
        out_shape=out_shape,
        scratch_shapes=[pltpu.VMEM((tc, d_rwkv), F32) for _ in range(8)]
        + [pltpu.VMEM((d_rwkv // LANES, LANES, LANES), F32)],
        compiler_params=pltpu.CompilerParams(
            dimension_semantics=("arbitrary", "arbitrary"), vmem_limit_bytes=VMEM_LIMIT),
        name="rwkv7_first" if first_layer else "rwkv7",
    )(*args)
    return (res[0], res[1]) if first_layer else (res[0], vfirst)


def _mix_ffn_kernel(a_ref, rw_ref, x_ref, gains_ref, ga_ref, wo_ref, wu_ref, wd_ref, out_ref,
                    acc_ref, *, ff_chunk):
    n_pairs = a_ref.shape[0]
    g_post, g_pre, g_ffn = (gains_ref[j:j + 1, :] for j in range(3))
    attn = jnp.concatenate([a_ref[p].astype(F32) for p in range(n_pairs)], axis=-1)
    attn = _rms(attn, ga_ref[...]).astype(BF16)
    d_attn = attn.shape[1]
    mixed = _mm(attn, wo_ref[0:d_attn, :]) + _mm(rw_ref[...], wo_ref[d_attn:, :])
    x1 = x_ref[...] + _rms(mixed, g_post)
    h = _rms(x1, g_pre).astype(BF16)
    d_ff = wu_ref.shape[1]
    for c in range(d_ff // ff_chunk):
        cs = slice(c * ff_chunk, (c + 1) * ff_chunk)
        u = jnp.maximum(_mm(h, wu_ref[:, cs]), 0.0)
        part = _mm((u * u).astype(BF16), wd_ref[cs, :])
        if c == 0:
            acc_ref[...] = part
        else:
            acc_ref[...] += part
    out_ref[...] = x1 + _rms(acc_ref[...], g_ffn)


def _mix_ffn(attn, rw, x2, gains, ga, wo, wu, wd, *, tm=512, ff_chunk=1024):
    t, d = x2.shape
    n_pairs = attn.shape[0]
    d_rwkv = rw.shape[1]
    row = lambda i: (i, 0)
    const = lambda i: (0, 0)
    resident = lambda shape: pl.BlockSpec(shape, const, pipeline_mode=pl.Buffered(1))
    return pl.pallas_call(
        functools.partial(_mix_ffn_kernel, ff_chunk=ff_chunk),
        grid=(t // tm,),
        in_specs=[pl.BlockSpec((n_pairs, tm, LANES), lambda i: (0, i, 0)),
                  pl.BlockSpec((tm, d_rwkv), row), pl.BlockSpec((tm, d), row),
                  pl.BlockSpec(gains.shape, const), pl.BlockSpec(ga.shape, const),
                  resident(wo.shape), resident(wu.shape), resident(wd.shape)],
        out_specs=pl.BlockSpec((tm, d), row),
        out_shape=jax.ShapeDtypeStruct((t, d), F32),
        scratch_shapes=[pltpu.VMEM((tm, d), F32)],
        compiler_params=pltpu.CompilerParams(
            dimension_semantics=("arbitrary",), vmem_limit_bytes=VMEM_LIMIT),
        name="mix_ffn",
    )(attn, rw, x2, gains, ga, wo, wu, wd)


def _layer_params(i, d_attn, d_rwkv, w_in_first, w_in_rest, mu_shift, mu_shift_mv,
                  decay_up, aaa_up, gate_up, mv_up):
    w = w_in_first if i == 0 else w_in_rest[i - 1]
    d = w.shape[0]
    n_first = w_in_first.shape[1]
    lora_used = LORA_DECAY + LORA_AAA + LORA_GATE + LORA_MV
    n_pairs = d_attn // LANES
    qkv = [w[:, j * d_attn + p * LANES:j * d_attn + (p + 1) * LANES]
           for p in range(n_pairs) for j in range(3)]
    pieces = qkv + [w[:, 3 * d_attn:n_first]]
    mus = [mu_shift[i]]
    if i == 0:
        pieces.append(jnp.zeros((d, LORA_MV), F32))
        mus.append(jnp.zeros((LORA_MV,), F32))
    else:
        pieces.append(w[:, n_first:])
        mus.append(mu_shift_mv[i - 1])
    pieces.append(jnp.zeros((d, LORA_PAD - lora_used), F32))
    mus.append(jnp.zeros((LORA_PAD - lora_used,), F32))
    w_r = jnp.concatenate(pieces, axis=1).astype(BF16)
    mu_r = jnp.concatenate(mus)[None, :]
    wl = jnp.zeros((LORA_PAD, 4 * d_rwkv), F32)
    o1 = LORA_DECAY
    o2 = o1 + LORA_AAA
    o3 = o2 + LORA_GATE
    wl = wl.at[0:o1, 0:d_rwkv].set(decay_up[i])
    wl = wl.at[o1:o2, d_rwkv:2 * d_rwkv].set(aaa_up[i])
    wl = wl.at[o2:o3, 2 * d_rwkv:3 * d_rwkv].set(gate_up[i])
    if i > 0:
        wl = wl.at[o3:o3 + LORA_MV, 3 * d_rwkv:4 * d_rwkv].set(mv_up[i - 1])
    return w_r, mu_r, wl.astype(BF16)


def kernel(x, norm_mix_pre, norm_mix_post, norm_ffn_pre, norm_ffn_post, w_in_first, w_in_rest,
           mu_shift, mu_shift_mv, attn_out_gain, decay_w0, decay_up, aaa_a0, aaa_up, mv_v0, mv_up,
           gate_up, k_k, k_a, r_k, gn_w, gn_b, w_out, w_ffn_up, w_ffn_down):
    batch, seq, d = x.shape
    depth = norm_mix_pre.shape[0]
    d_attn = attn_out_gain.shape[1]
    d_rwkv = decay_w0.shape[1]
    x2 = x.reshape(batch * seq, d)
    vfirst = None
    for i in range(depth):
        w_r, mu_r, wl = _layer_params(i, d_attn, d_rwkv, w_in_first, w_in_rest, mu_shift,
                                      mu_shift_mv, decay_up, aaa_up, gate_up, mv_up)
        mv0 = mv_v0[i - 1] if i > 0 else jnp.zeros((d_rwkv,), F32)
        vecs = jnp.stack([decay_w0[i], aaa_a0[i], mv0, k_k[i], k_a[i], r_k[i].reshape(-1),
                          gn_w[i], gn_b[i]])
        x1, x4, x16, zr, zl = _inproj(x2, norm_mix_pre[i][None, :], w_r, mu_r, batch=batch, seq=seq)
        attn = _attention(x1, x4, x16, batch=batch, seq=seq)
        rw, vfirst = _rwkv(zr, zl, vfirst, wl, vecs, batch=batch, seq=seq)
        gains = jnp.stack([norm_mix_post[i], norm_ffn_pre[i], norm_ffn_post[i]])
        x2 = _mix_ffn(attn, rw, x2, gains, attn_out_gain[i][None, :], w_out[i].astype(BF16),
                      w_ffn_up[i].astype(BF16), w_ffn_down[i].astype(BF16))
    return x2.reshape(batch, seq, d)
```

```python
import functools
import math

import jax
import jax.numpy as jnp
from jax import lax
from jax.experimental import pallas as pl
from jax.experimental.pallas import tpu as pltpu

F32 = jnp.float32
BF16 = jnp.bfloat16

HEAD_DIM = 64
LANES = 128
NORM_EPS = 1e-6
GN_EPS = 64e-5
DILATIONS = (1, 4, 16)
BAND = 128
QKV_W = 3 * LANES
INPROJ_TM = BAND * DILATIONS[1]
ATTN_GROUP = 4
LORA_DECAY, LORA_AAA, LORA_GATE, LORA_MV = 32, 32, 96, 32
LORA_PAD = 256
CHUNK = 64
RWKV_TC = 512
RWKV_SUB = 256
NEG_BIG = -1e30
VMEM_LIMIT = 56 * 1024 * 1024


def _nt(a, b):
    return lax.dot_general(a, b, (((1,), (1,)), ((), ())), preferred_element_type=F32)


def _tn(a, b):
    return lax.dot_general(a, b, (((0,), (0,)), ((), ())), preferred_element_type=F32)


def _mm(a, b):
    return jnp.dot(a, b, preferred_element_type=F32)


def _rms(x, g):
    return x * lax.rsqrt(jnp.mean(x * x, axis=-1, keepdims=True) + NORM_EPS) * g


def _inproj_kernel(x_ref, g_ref, w_ref, mu_ref, x1_ref, x4_ref, x16_ref, zr_ref, zl_ref,
                   carry_ref, zs_ref, z4_ref, *, seq_tiles, n_pairs, d_rwkv3):
    i = pl.program_id(0)
    tm = x_ref.shape[0]
    h = _rms(x_ref[...], g_ref[...]).astype(BF16)
    for p in range(n_pairs):
        if p % 2 == 0:
            zz = _mm(h, w_ref[:, p * QKV_W:(p + 2) * QKV_W])
        z = zz[:, (p % 2) * QKV_W:(p % 2 + 1) * QKV_W]
        x1_ref[p, :, 0] = z.astype(BF16).reshape(tm // BAND, BAND, QKV_W)
        f = DILATIONS[1]
        for j in range(QKV_W // LANES):
            ls = slice(j * LANES, (j + 1) * LANES)
            zs_ref[p, j] = z[:, ls]
            for b in range(f):
                z4 = zs_ref[p, j, pl.ds(b, tm // f, stride=f), :]
                x4_ref[p, b, :, ls] = z4.astype(BF16)
                z4_ref[p, j, b] = z4
                for a in range(f):
                    z16 = z4_ref[p, j, b, pl.ds(a, tm // (f * f), stride=f), :]
                    x16_ref[p, f * a + b, :, ls] = z16.astype(BF16)
    d_attn3 = n_pairs * QKV_W
    first_tile = (i % seq_tiles) == 0
    row0 = lax.broadcasted_iota(jnp.int32, (tm, 1), 0) == 0
    n_shift = d_rwkv3 + LORA_PAD
    cw = 512
    c0 = 0
    while c0 < n_shift:
        w = min(cw, n_shift - c0)
        z = _mm(h, w_ref[:, d_attn3 + c0:d_attn3 + c0 + w])
        carry = jnp.where(first_tile, 0.0, carry_ref[:, c0:c0 + w])
        prev = jnp.where(row0, carry, pltpu.roll(z, 1, 0))
        carry_ref[:, c0:c0 + w] = z[tm - 1:tm, :]
        zs = z + (prev - z) * mu_ref[:, c0:c0 + w]
        if c0 < d_rwkv3:
            zr_ref[:, c0:c0 + w] = zs
        else:
            zl_ref[...] = zs
        c0 += w


def _inproj(x2, g, w, mu, *, batch, seq):
    tm = INPROJ_TM
    t, d = x2.shape
    nc = w.shape[1]
    n_pairs = (d // 2) // LANES
    d_attn3 = n_pairs * QKV_W
    d_rwkv3 = nc - d_attn3 - LORA_PAD
    seq_tiles = seq // tm
    d4, d16 = DILATIONS[1], DILATIONS[2]
    assert tm == BAND * d4 and d16 == d4 * d4 and seq % (BAND * d16) == 0
    sub16 = BAND * d16 // tm
    kern = functools.partial(_inproj_kernel, seq_tiles=seq_tiles, n_pairs=n_pairs, d_rwkv3=d_rwkv3)
    return pl.pallas_call(
        kern,
        grid=(t // tm,),
        in_specs=[
            pl.BlockSpec((tm, d), lambda i: (i, 0)),
            pl.BlockSpec((1, d), lambda i: (0, 0)),
            pl.BlockSpec((d, nc), lambda i: (0, 0)),
            pl.BlockSpec((1, nc - d_attn3), lambda i: (0, 0)),
        ],
        out_specs=[
            pl.BlockSpec((n_pairs, None, tm // BAND, 1, BAND, QKV_W),
                         lambda i: (0, i // seq_tiles, i % seq_tiles, 0, 0, 0)),
            pl.BlockSpec((n_pairs, None, None, d4, BAND, QKV_W),
                         lambda i: (0, i // seq_tiles, i % seq_tiles, 0, 0, 0)),
            pl.BlockSpec((n_pairs, None, None, d16, tm // d16, QKV_W),
                         lambda i: (0, i // seq_tiles, (i % seq_tiles) // sub16, 0,
                                    (i % seq_tiles) % sub16, 0)),
            pl.BlockSpec((tm, d_rwkv3), lambda i: (i, 0)),
            pl.BlockSpec((tm, LORA_PAD), lambda i: (i, 0)),
        ],
        out_shape=[
            jax.ShapeDtypeStruct((n_pairs, batch, seq // BAND, 1, BAND, QKV_W), BF16),
            jax.ShapeDtypeStruct((n_pairs, batch, seq // (BAND * d4), d4, BAND, QKV_W), BF16),
            jax.ShapeDtypeStruct((n_pairs, batch, seq // (BAND * d16), d16, BAND, QKV_W), BF16),
            jax.ShapeDtypeStruct((t, d_rwkv3), F32),
            jax.ShapeDtypeStruct((t, LORA_PAD), F32),
        ],
        scratch_shapes=[pltpu.VMEM((1, nc - d_attn3), F32),
                        pltpu.VMEM((n_pairs, QKV_W // LANES, tm, LANES), F32),
                        pltpu.VMEM((n_pairs, QKV_W // LANES, d4, tm // d4, LANES), F32)],
        compiler_params=pltpu.CompilerParams(
            dimension_semantics=("arbitrary",), vmem_limit_bytes=VMEM_LIMIT),
        name="inproj",
    )(x2, g, w, mu)


def _attn_scores(x_ref, probs, head0):
    scale = 1.0 / math.sqrt(HEAD_DIM)
    out = []
    for r, n in probs:
        prev = jnp.maximum(n - 1, 0)
        q = x_ref[n, r, :, 0:LANES] * scale
        kw = jnp.concatenate([x_ref[prev, r, :, LANES:2 * LANES], x_ref[n, r, :, LANES:2 * LANES]],
                             axis=0)
        out.append(_nt(_block_diag(q, head0), kw))
    return out


def _attn_softmax(probs, scores, consts):
    head0, bias_first, bias_rest = consts
    s = []
    for (r, n), si in zip(probs, scores):
        bias = jnp.where(n == 0, bias_first, bias_rest)
        s.append(jnp.concatenate([si[:BAND] + bias, si[BAND:] + bias], axis=0))
    m = [jnp.max(si, axis=-1, keepdims=True) for si in s]
    p = [jnp.exp(si - mi).astype(BF16) for si, mi in zip(s, m)]
    return p, [jnp.where(head0, mi[:BAND], mi[BAND:]) for mi in m]


def _attn_pv(x_ref, probs, p, head0):
    ones = jnp.ones((2 * BAND, LANES), BF16)
    pv = []
    for (r, n), pi in zip(probs, p):
        prev = jnp.maximum(n - 1, 0)
        vw = jnp.concatenate([x_ref[prev, r, :, 2 * LANES:], x_ref[n, r, :, 2 * LANES:]], axis=0)
        pv.append(_mm(pi, jnp.concatenate([vw, ones], axis=1)))
    return [tuple(jnp.where(head0, t[:BAND], t[BAND:]) for t in (pvi[:, :LANES], pvi[:, LANES:]))
            for pvi in pv]


def _attn_kernel(x1_ref, x4_ref, x16_ref, out_ref, nat_s, ph_s, mid_s, p_scr):
    seq = out_ref.shape[0]
    lane = lax.broadcasted_iota(jnp.int32, (1, LANES), 1)
    head0 = lane < HEAD_DIM
    qrow = lax.broadcasted_iota(jnp.int32, (BAND, 1), 0)
    kcol = lax.broadcasted_iota(jnp.int32, (1, 2 * BAND), 1)
    rel = BAND + qrow - kcol
    bias_rest = jnp.where((rel >= 0) & (rel <= BAND), 0.0, NEG_BIG).astype(F32)
    bias_first = jnp.where(kcol < BAND, NEG_BIG, bias_rest)
    consts = (head0, bias_first, bias_rest)
    group = ATTN_GROUP
    n_groups = seq // (BAND * group)

    for bi, (d, x_ref) in reversed(list(enumerate(zip(DILATIONS, (x1_ref, x4_ref, x16_ref))))):
        n_blocks = seq // (BAND * d)
        shift = n_blocks.bit_length() - 1
        sub_len = seq // d
        dst = nat_s.at[bi] if d == 1 else ph_s

        def probs_of(g, shift=shift, n_blocks=n_blocks):
            js = [g * group + i for i in range(group)]
            return [(j >> shift, j & (n_blocks - 1)) for j in js], js

        def store(js, q, vals, dst=dst):
            for j, t in zip(js, vals):
                dst[q, pl.ds(pl.multiple_of(j * BAND, BAND), BAND), :] = t

        probs0, js0 = probs_of(jnp.int32(0))
        p0, m0 = _attn_softmax(probs0, _attn_scores(x_ref, probs0, head0), consts)
        store(js0, 1, m0)
        for i, pi in enumerate(p0):
            p_scr[i] = pi

        def body(g, carry, x_ref=x_ref, probs_of=probs_of, store=store):
            nxt, js_next = probs_of(jnp.minimum(g + 1, n_groups - 1))
            s_next = _attn_scores(x_ref, nxt, head0)
            cur, js = probs_of(g)
            acc_l = _attn_pv(x_ref, cur, [p_scr[i] for i in range(group)], head0)
            p_next, m_next = _attn_softmax(nxt, s_next, consts)
            store(js, 0, [t[0] for t in acc_l])
            store(js, 2, [t[1] for t in acc_l])
            store(js_next, 1, m_next)
            for i, pi in enumerate(p_next):
                p_scr[i] = pi
            return carry

        lax.fori_loop(0, n_groups, body, 0, unroll=True)
        f = DILATIONS[1]
        if d == f:
            for r in range(d):
                staged = slice(r * sub_len, (r + 1) * sub_len)
                for q in range(3):
                    nat_s[bi, q, pl.ds(r, sub_len, stride=f), :] = ph_s[q, staged, :]
        elif d == f * f:
            for q in range(3):
                for b in range(f):
                    for a in range(f):
                        staged = slice((f * a + b) * sub_len, (f * a + b + 1) * sub_len)
                        mid_s[pl.ds(b * f * sub_len + a, sub_len, stride=f), :] = ph_s[q, staged, :]
                for b in range(f):
                    rows = slice(b * f * sub_len, (b + 1) * f * sub_len)
                    nat_s[bi, q, pl.ds(b, f * sub_len, stride=f), :] = mid_s[rows, :]

    rows_c = 512
    for c in range(seq // rows_c):
        rows = slice(c * rows_c, (c + 1) * rows_c)
        ms = [nat_s[bi, 1, rows, :] for bi in range(3)]
        top = jnp.maximum(jnp.maximum(ms[0], ms[1]), ms[2])
        es = [jnp.exp(mb - top) for mb in ms]
        num = sum(e * nat_s[bi, 0, rows, :] for bi, e in enumerate(es))
        den = sum(e * nat_s[bi, 2, rows, :] for bi, e in enumerate(es))
        out_ref[rows, :] = (num / den).astype(out_ref.dtype)


def _attention(x1, x4, x16, *, batch, seq):
    n_pairs = x1.shape[0]
    spec = lambda x: pl.BlockSpec((None, None) + x.shape[2:], lambda b, p: (p, b, 0, 0, 0, 0))
    return pl.pallas_call(
        _attn_kernel,
        grid=(batch, n_pairs),
        in_specs=[spec(x1), spec(x4), spec(x16)],
        out_specs=pl.BlockSpec((None, seq, LANES), lambda b, p: (p, b, 0)),
        out_shape=jax.ShapeDtypeStruct((n_pairs, batch * seq, LANES), BF16),
        scratch_shapes=[pltpu.VMEM((3, 3, seq, LANES), F32), pltpu.VMEM((3, seq, LANES), F32),
                        pltpu.VMEM((seq, LANES), F32),
                        pltpu.VMEM((ATTN_GROUP, 2 * BAND, 2 * BAND), BF16)],
        compiler_params=pltpu.CompilerParams(
            dimension_semantics=("arbitrary", "arbitrary"), vmem_limit_bytes=VMEM_LIMIT),
        name="attention",
    )(x1, x4, x16)


def _block_diag(x, head0):
    zero = jnp.zeros_like(x)
    return jnp.concatenate([jnp.where(head0, x, zero), jnp.where(head0, zero, x)], axis=0)


def _chunk_consts():
    c = CHUNK
    trow = lax.broadcasted_iota(jnp.int32, (c, 1), 0)
    col3 = lax.broadcasted_iota(jnp.int32, (1, 3 * c), 1)
    tri3 = jnp.where((col3 & (c - 1)) <= trow, 1.0, 0.0).astype(BF16)
    lane_w = lax.broadcasted_iota(jnp.int32, (1, LANES), 1)
    scol = lane_w & (HEAD_DIM - 1)
    strict = scol < trow
    incl = scol <= trow
    eye_w = jnp.where(scol == trow, 1.0, 0.0).astype(F32)
    head0 = lane_w < HEAD_DIM
    rr = lax.broadcasted_iota(jnp.int32, (LANES, 1), 0)
    same_head = (rr // HEAD_DIM) == (lane_w // HEAD_DIM)
    levels = []
    s = 1
    while s < c:
        in_pair = (trow // (2 * s)) == (scol // (2 * s))
        levels.append(in_pair & ((trow // s) != (scol // s)) & strict)
        s *= 2
    return dict(tri3=tri3, strict=strict, incl=incl, eye_w=eye_w, head0=head0,
                same_head=same_head, levels=levels)


def _chunk_precompute(ins, consts):
    c = CHUNK
    head0 = consts["head0"]
    strict, incl, same_head = consts["strict"], consts["incl"], consts["same_head"]
    levels = consts["levels"]
    zero = jnp.zeros((c, LANES), F32)
    zero_sq = jnp.zeros((LANES, LANES), F32)

    def bd(x):
        return _block_diag(x, head0).astype(BF16)

    splits = []
    for (_, lw, _, _, _, _) in ins:
        hi = lw.astype(BF16)
        r1 = lw - hi.astype(F32)
        mid = r1.astype(BF16)
        lo = (r1 - mid.astype(F32)).astype(BF16)
        splits.append(jnp.concatenate([hi, mid, lo], axis=0))
    cums = []
    for i in range(0, len(splits), 2):
        both = _mm(consts["tri3"], jnp.concatenate(splits[i:i + 2], axis=1))
        cums += [both[:, :LANES], both[:, LANES:]]
    yield
    ops = []
    for (r, lw, k, v, kk, b), cum in zip(ins, cums):
        cum_end = cum[c - 1:c, :]
        e_neg = jnp.exp(-cum)
        e_end = jnp.exp(cum_end - cum)
        ops.append(dict(kq=kk * jnp.exp(cum - lw), rq=r * jnp.exp(cum), kt=k * e_neg, bt=b * e_neg,
                        kh=(k * e_end).astype(BF16), bh=(b * e_end).astype(BF16),
                        g=jnp.exp(cum_end), v=v))
    a_all = [_nt(jnp.concatenate([o["kq"], o["rq"]], axis=0).astype(BF16),
                 jnp.concatenate([bd(o["bt"]), bd(o["kt"])], axis=0)) for o in ops]
    a_ab = [jnp.where(strict, a[:c, :LANES], zero) for a in a_all]
    a_rb = [jnp.where(incl, a[c:, :LANES], zero).astype(BF16) for a in a_all]
    av = [_mm(jnp.concatenate([jnp.where(strict, a[:c, LANES:], zero),
                               jnp.where(incl, a[c:, LANES:], zero)], axis=0).astype(BF16),
              bd(o["v"])) for a, o in zip(a_all, ops)]
    yield
    t_inv = [consts["eye_w"] - jnp.where(levels[0], a, zero) for a in a_ab]
    for mask in levels[1:]:
        x = [_mm(jnp.where(mask, a, zero).astype(BF16), bd(t)) for a, t in zip(a_ab, t_inv)]
        yield
        t_inv = [t - _mm(t.astype(BF16), bd(xi)) for t, xi in zip(t_inv, x)]
        yield
    twu = [_mm(t.astype(BF16), jnp.concatenate([bd(o["kq"]), bd(a[:c])], axis=1))
           for t, o, a in zip(t_inv, ops, av)]
    yield
    arb = [_mm(ar, jnp.concatenate([bd(x[:, :LANES]), bd(x[:, LANES:])], axis=1))
           for ar, x in zip(a_rb, twu)]
    yield
    out = []
    for o, a, x, y in zip(ops, av, twu, arb):
        w = x[:, :LANES].astype(BF16)
        u0 = x[:, LANES:]
        pm = -jnp.where(same_head, _tn(w, o["bh"]), zero_sq)
        dd = jnp.where(same_head,
                       _tn(jnp.concatenate([o["v"], -u0], axis=0).astype(BF16),
                           jnp.concatenate([o["kh"], o["bh"]], axis=0)), zero_sq)
        out.append((o["rq"] - y[:, :LANES], a[c:] - y[:, LANES:], pm, dd, o["g"]))
    return out


def _spread(fillers, every=2):
    out = []
    for f in fillers:
        out.append(f)
        out.extend([lambda: None] * (every - 1))
    return out


def _interleave(gen, fillers):
    fillers = list(fillers)
    while True:
        try:
            next(gen)
        except StopIteration as stop:
            result = stop.value
            break
        if fillers:
            fillers.pop(0)()
    for f in fillers:
        f()
    return result


def _chunk_apply(pre, states):
    sb = [s.astype(BF16) for s in states]
    ys = [_nt(p[0].astype(BF16), s) + p[1] for p, s in zip(pre, sb)]
    new = [st * p[4] + _mm(s, p[2].astype(BF16)) + p[3] for p, s, st in zip(pre, sb, states)]
    return ys, new


def _rwkv_kernel(*refs, first_layer, d_rwkv):
    if first_layer:
        (zr_ref, zl_ref, wl_ref, vec_ref, out_ref, vfirst_out_ref,
         r_s, lw_s, k_s, v_s, kk_s, b_s, g_s, y_s, state_s) = refs
        vfirst_ref = None
    else:
        (zr_ref, zl_ref, vfirst_ref, wl_ref, vec_ref, out_ref,
         r_s, lw_s, k_s, v_s, kk_s, b_s, g_s, y_s, state_s) = refs
    tc = zr_ref.shape[0]
    n_pairs = d_rwkv // LANES
    c = CHUNK
    sub = RWKV_SUB
    sub_chunks = sub // c

    @pl.when(pl.program_id(1) == 0)
    def _():
        state_s[...] = jnp.zeros_like(state_s)

    w0, a0, mv0, k_k, k_a, r_k, gn_w, gn_b = (vec_ref[j:j + 1, :] for j in range(8))
    hrow = lax.broadcasted_iota(jnp.int32, (2 * LANES, 1), 0)
    hcol = lax.broadcasted_iota(jnp.int32, (1, 2 * LANES), 1)
    same = jnp.where((hrow // HEAD_DIM) == (hcol // HEAD_DIM), 1.0, 0.0).astype(BF16)

    def head_sum(x):
        xb = x.astype(BF16)
        w = 2 * LANES
        return jnp.concatenate([_mm(xb[:, j * w:(j + 1) * w], same) for j in range(d_rwkv // w)],
                               axis=1)

    lane = lax.broadcasted_iota(jnp.int32, (1, LORA_PAD), 1)
    o1 = LORA_DECAY
    o2 = o1 + LORA_AAA
    o3 = o2 + LORA_GATE

    def prepare(rows):
        zl = zl_ref[rows, :]
        act = jnp.where(lane < o1, jnp.tanh(zl),
                        jnp.where((lane >= o2) & (lane < o3), jax.nn.sigmoid(zl), zl))
        lora = _mm(act.astype(BF16), wl_ref[...])
        r = zr_ref[rows, 0:d_rwkv]
        k = zr_ref[rows, d_rwkv:2 * d_rwkv]
        v = zr_ref[rows, 2 * d_rwkv:3 * d_rwkv]
        lw = -math.exp(-0.5) * jax.nn.sigmoid(w0 + lora[:, 0:d_rwkv])
        a = jax.nn.sigmoid(a0 + lora[:, d_rwkv:2 * d_rwkv])
        if first_layer:
            vfirst_out_ref[rows, :] = v
        else:
            vgate = jax.nn.sigmoid(mv0 + lora[:, 3 * d_rwkv:4 * d_rwkv])
            v = v + (vfirst_ref[rows, :] - v) * vgate
        kk = k * k_k
        kk = kk / jnp.maximum(jnp.sqrt(head_sum(kk * kk)), 1e-12)
        r_s[rows, :] = r
        lw_s[rows, :] = lw
        k_s[rows, :] = k * (1.0 + (a - 1.0) * k_a)
        v_s[rows, :] = v
        kk_s[rows, :] = kk
        b_s[rows, :] = kk * a
        g_s[rows, :] = lora[:, 2 * d_rwkv:3 * d_rwkv]

    def problems(h):
        ins = []
        for ci in range(h * sub_chunks, (h + 1) * sub_chunks):
            rows = slice(ci * c, (ci + 1) * c)
            for p in range(n_pairs):
                ls = slice(p * LANES, (p + 1) * LANES)
                ins.append(tuple(ref[rows, ls] for ref in (r_s, lw_s, k_s, v_s, kk_s, b_s)))
        return ins

    states = [state_s[p] for p in range(n_pairs)]

    def chain_steps(h, pre):
        def step(j):
            def run():
                ys, new = _chunk_apply(pre[j * n_pairs:(j + 1) * n_pairs], states)
                states[:] = new
                ci = h * sub_chunks + j
                for p in range(n_pairs):
                    y_s[ci * c:(ci + 1) * c, p * LANES:(p + 1) * LANES] = ys[p]
            return run

        rows = slice(h * sub, (h + 1) * sub)
        tmp = {}

        def stats():
            y = y_s[rows, :]
            tmp["dlt"] = y - head_sum(y) * (1.0 / HEAD_DIM)
            tmp["bonus"] = head_sum(r_s[rows, :] * k_s[rows, :] * r_k) * v_s[rows, :]

        def finish():
            dlt = tmp["dlt"]
            var = head_sum(dlt * dlt) * (1.0 / HEAD_DIM)
            yn = dlt * lax.rsqrt(var + GN_EPS) * gn_w + gn_b
            out_ref[rows, :] = ((yn + tmp["bonus"]) * g_s[rows, :]).astype(out_ref.dtype)

        return [step(j) for j in range(sub_chunks)] + [stats, finish]

    consts = _chunk_consts()
    n_sub = tc // sub
    for h in range(n_sub):
        prepare(slice(h * sub, (h + 1) * sub))
    pending = []
    for h in range(n_sub):
        pre = _interleave(_chunk_precompute(problems(h), consts), _spread(pending))
        pending = chain_steps(h, pre)
    for f in pending:
        f()
    for p in range(n_pairs):
        state_s[p] = states[p]


def _rwkv(zr, zl, vfirst, wl, vecs, *, batch, seq, tc=RWKV_TC):
    t = zr.shape[0]
    d_rwkv = zr.shape[1] // 3
    first_layer = vfirst is None
    tiles = seq // tc
    row = lambda b, i: (b * tiles + i, 0)
    const = lambda b, i: (0, 0)
    in_specs = [pl.BlockSpec((tc, 3 * d_rwkv), row), pl.BlockSpec((tc, LORA_PAD), row)]
    args = [zr, zl]
    if not first_layer:
        in_specs.append(pl.BlockSpec((tc, d_rwkv), row))
        args.append(vfirst)
    in_specs += [pl.BlockSpec(wl.shape, const), pl.BlockSpec(vecs.shape, const)]
    args += [wl, vecs]
    out_specs = [pl.BlockSpec((tc, d_rwkv), row)]
    out_shape = [jax.ShapeDtypeStruct((t, d_rwkv), BF16)]
    if first_layer:
        out_specs.append(pl.BlockSpec((tc, d_rwkv), row))
        out_shape.append(jax.ShapeDtypeStruct((t, d_rwkv), F32))
    kern = functools.partial(_rwkv_kernel, first_layer=first_layer, d_rwkv=d_rwkv)
    res = pl.pallas_call(
        kern,
        grid=(batch, tiles),
        in_specs=in_specs,
        out_specs=out_specs,
        out_shape=out_shape,
        scratch_shapes=[pltpu.VMEM((tc, d_rwkv), F32) for _ in range(8)]
        + [pltpu.VMEM((d_rwkv // LANES, LANES, LANES), F32)],
        compiler_params=pltpu.CompilerParams(
            dimension_semantics=("arbitrary", "arbitrary"), vmem_limit_bytes=VMEM_LIMIT),
        name="rwkv7_first" if first_layer else "rwkv7",
    )(*args)
    return (res[0], res[1]) if first_layer else (res[0], vfirst)


def _mix_ffn_kernel(a_ref, rw_ref, x_ref, gains_ref, ga_ref, wo_ref, wu_ref, wd_ref, out_ref,
                    acc_ref, *, ff_chunk):
    n_pairs = a_ref.shape[0]
    tm = x_ref.shape[0]
    g_post, g_pre, g_ffn = (gains_ref[j:j + 1, :] for j in range(3))
    d_attn = n_pairs * LANES
    d_ff = wu_ref.shape[1]
    halves = [slice(j * (tm // 2), (j + 1) * (tm // 2)) for j in range(2)]
    x1s, hs = [], []
    for rows in halves:
        attn = jnp.concatenate([a_ref[p, rows, :].astype(F32) for p in range(n_pairs)], axis=-1)
        attn = _rms(attn, ga_ref[...]).astype(BF16)
        mixed = _mm(attn, wo_ref[0:d_attn, :]) + _mm(rw_ref[rows, :], wo_ref[d_attn:, :])
        x1 = x_ref[rows, :] + _rms(mixed, g_post)
        x1s.append(x1)
        hs.append(_rms(x1, g_pre).astype(BF16))
    for rows, h in zip(halves, hs):
        for c in range(d_ff // ff_chunk):
            cs = slice(c * ff_chunk, (c + 1) * ff_chunk)
            u = jnp.maximum(_mm(h, wu_ref[:, cs]), 0.0)
            part = _mm((u * u).astype(BF16), wd_ref[cs, :])
            if c == 0:
                acc_ref[rows, :] = part
            else:
                acc_ref[rows, :] += part
    for rows, x1 in zip(halves, x1s):
        out_ref[rows, :] = x1 + _rms(acc_ref[rows, :], g_ffn)


def _mix_ffn(attn, rw, x2, gains, ga, wo, wu, wd, *, tm=512, ff_chunk=1024):
    t, d = x2.shape
    n_pairs = attn.shape[0]
    d_rwkv = rw.shape[1]
    row = lambda i: (i, 0)
    const = lambda i: (0, 0)
    resident = lambda shape: pl.BlockSpec(shape, const, pipeline_mode=pl.Buffered(1))
    return pl.pallas_call(
        functools.partial(_mix_ffn_kernel, ff_chunk=ff_chunk),
        grid=(t // tm,),
        in_specs=[pl.BlockSpec((n_pairs, tm, LANES), lambda i: (0, i, 0)),
                  pl.BlockSpec((tm, d_rwkv), row), pl.BlockSpec((tm, d), row),
                  pl.BlockSpec(gains.shape, const), pl.BlockSpec(ga.shape, const),
                  resident(wo.shape), resident(wu.shape), resident(wd.shape)],
        out_specs=pl.BlockSpec((tm, d), row),
        out_shape=jax.ShapeDtypeStruct((t, d), F32),
        scratch_shapes=[pltpu.VMEM((tm, d), F32)],
        compiler_params=pltpu.CompilerParams(
            dimension_semantics=("arbitrary",), vmem_limit_bytes=VMEM_LIMIT),
        name="mix_ffn",
    )(attn, rw, x2, gains, ga, wo, wu, wd)


def _layer_params(i, d_attn, d_rwkv, w_in_first, w_in_rest, mu_shift, mu_shift_mv,
                  decay_up, aaa_up, gate_up, mv_up):
    w = w_in_first if i == 0 else w_in_rest[i - 1]
    d = w.shape[0]
    n_first = w_in_first.shape[1]
    lora_used = LORA_DECAY + LORA_AAA + LORA_GATE + LORA_MV
    n_pairs = d_attn // LANES
    qkv = [w[:, j * d_attn + p * LANES:j * d_attn + (p + 1) * LANES]
           for p in range(n_pairs) for j in range(3)]
    pieces = qkv + [w[:, 3 * d_attn:n_first]]
    mus = [mu_shift[i]]
    if i == 0:
        pieces.append(jnp.zeros((d, LORA_MV), F32))
        mus.append(jnp.zeros((LORA_MV,), F32))
    else:
        pieces.append(w[:, n_first:])
        mus.append(mu_shift_mv[i - 1])
    pieces.append(jnp.zeros((d, LORA_PAD - lora_used), F32))
    mus.append(jnp.zeros((LORA_PAD - lora_used,), F32))
    w_r = jnp.concatenate(pieces, axis=1).astype(BF16)
    mu_r = jnp.concatenate(mus)[None, :]
    wl = jnp.zeros((LORA_PAD, 4 * d_rwkv), F32)
    o1 = LORA_DECAY
    o2 = o1 + LORA_AAA
    o3 = o2 + LORA_GATE
    wl = wl.at[0:o1, 0:d_rwkv].set(decay_up[i])
    wl = wl.at[o1:o2, d_rwkv:2 * d_rwkv].set(aaa_up[i])
    wl = wl.at[o2:o3, 2 * d_rwkv:3 * d_rwkv].set(gate_up[i])
    if i > 0:
        wl = wl.at[o3:o3 + LORA_MV, 3 * d_rwkv:4 * d_rwkv].set(mv_up[i - 1])
    return w_r, mu_r, wl.astype(BF16)


def kernel(x, norm_mix_pre, norm_mix_post, norm_ffn_pre, norm_ffn_post, w_in_first, w_in_rest,
           mu_shift, mu_shift_mv, attn_out_gain, decay_w0, decay_up, aaa_a0, aaa_up, mv_v0, mv_up,
           gate_up, k_k, k_a, r_k, gn_w, gn_b, w_out, w_ffn_up, w_ffn_down):
    batch, seq, d = x.shape
    depth = norm_mix_pre.shape[0]
    d_attn = attn_out_gain.shape[1]
    d_rwkv = decay_w0.shape[1]
    x2 = x.reshape(batch * seq, d)
    vfirst = None
    for i in range(depth):
        w_r, mu_r, wl = _layer_params(i, d_attn, d_rwkv, w_in_first, w_in_rest, mu_shift,
                                      mu_shift_mv, decay_up, aaa_up, gate_up, mv_up)
        mv0 = mv_v0[i - 1] if i > 0 else jnp.zeros((d_rwkv,), F32)
        vecs = jnp.stack([decay_w0[i], aaa_a0[i], mv0, k_k[i], k_a[i], r_k[i].reshape(-1),
                          gn_w[i], gn_b[i]])
        x1, x4, x16, zr, zl = _inproj(x2, norm_mix_pre[i][None, :], w_r, mu_r, batch=batch, seq=seq)
        attn = _attention(x1, x4, x16, batch=batch, seq=seq)
        rw, vfirst = _rwkv(zr, zl, vfirst, wl, vecs, batch=batch, seq=seq)
        gains = jnp.stack([norm_mix_post[i], norm_ffn_pre[i], norm_ffn_post[i]])
        x2 = _mix_ffn(attn, rw, x2, gains, attn_out_gain[i][None, :], w_out[i].astype(BF16),
                      w_ffn_up[i].astype(BF16), w_ffn_down[i].astype(BF16))
    return x2.reshape(batch, seq, d)
```

```python
import functools
import math

import jax
import jax.numpy as jnp
from jax import lax
from jax.experimental import pallas as pl
from jax.experimental.pallas import tpu as pltpu

F32 = jnp.float32
BF16 = jnp.bfloat16

HEAD_DIM = 64
LANES = 128
NORM_EPS = 1e-6
GN_EPS = 64e-5
DILATIONS = (1, 4, 16)
BAND = 128
QKV_W = 3 * LANES
INPROJ_TM = BAND * DILATIONS[1]
ATTN_GROUP = 1
LORA_DECAY, LORA_AAA, LORA_GATE, LORA_MV = 32, 32, 96, 32
LORA_PAD = 256
CHUNK = 64
RWKV_TC = 1024
RWKV_SUB = 256
NEG_BIG = -1e30
VMEM_LIMIT = 56 * 1024 * 1024


def _nt(a, b):
    return lax.dot_general(a, b, (((1,), (1,)), ((), ())), preferred_element_type=F32)


def _tn(a, b):
    return lax.dot_general(a, b, (((0,), (0,)), ((), ())), preferred_element_type=F32)


def _mm(a, b):
    return jnp.dot(a, b, preferred_element_type=F32)


def _rms(x, g):
    return x * lax.rsqrt(jnp.mean(x * x, axis=-1, keepdims=True) + NORM_EPS) * g


def _inproj_kernel(x_ref, g_ref, w_ref, mu_ref, x1_ref, x4_ref, x16_ref, zr_ref, zl_ref,
                   carry_ref, zs_ref, z4_ref, *, seq_tiles, n_pairs, d_rwkv3):
    i = pl.program_id(0)
    tm = x_ref.shape[0]
    h = _rms(x_ref[...], g_ref[...]).astype(BF16)
    d_attn3 = n_pairs * QKV_W
    f = DILATIONS[1]

    def attn_pairs(p0):
        zz = _mm(h, w_ref[:, p0 * QKV_W:(p0 + 2) * QKV_W])
        for p in (p0, p0 + 1):
            z = zz[:, (p - p0) * QKV_W:(p - p0 + 1) * QKV_W]
            x1_ref[p, :, 0] = z.astype(BF16).reshape(tm // BAND, BAND, QKV_W)
            for j in range(QKV_W // LANES):
                ls = slice(j * LANES, (j + 1) * LANES)
                zs_ref[p, j] = z[:, ls]
                for b in range(f):
                    z4 = zs_ref[p, j, pl.ds(b, tm // f, stride=f), :]
                    x4_ref[p, b, :, ls] = z4.astype(BF16)
                    z4_ref[p, j, b] = z4
                    for a in range(f):
                        z16 = z4_ref[p, j, b, pl.ds(a, tm // (f * f), stride=f), :]
                        x16_ref[p, f * a + b, :, ls] = z16.astype(BF16)

    first_tile = (i % seq_tiles) == 0
    row0 = lax.broadcasted_iota(jnp.int32, (tm, 1), 0) == 0

    def shifted(c0, w):
        z = _mm(h, w_ref[:, d_attn3 + c0:d_attn3 + c0 + w])
        carry = jnp.where(first_tile, 0.0, carry_ref[:, c0:c0 + w])
        prev = jnp.where(row0, carry, pltpu.roll(z, 1, 0))
        carry_ref[:, c0:c0 + w] = z[tm - 1:tm, :]
        zs = z + (prev - z) * mu_ref[:, c0:c0 + w]
        if c0 < d_rwkv3:
            zr_ref[:, c0:c0 + w] = zs
        else:
            zl_ref[...] = zs

    n_shift = d_rwkv3 + LORA_PAD
    cw = 512
    shift_tasks = [(c0, min(cw, n_shift - c0)) for c0 in range(0, n_shift, cw)]
    attn_tasks = list(range(0, n_pairs, 2))
    while attn_tasks or shift_tasks:
        if attn_tasks:
            attn_pairs(attn_tasks.pop(0))
        for _ in range(2):
            if shift_tasks:
                shifted(*shift_tasks.pop(0))


def _inproj(x2, g, w, mu, *, batch, seq):
    tm = INPROJ_TM
    t, d = x2.shape
    nc = w.shape[1]
    n_pairs = (d // 2) // LANES
    d_attn3 = n_pairs * QKV_W
    d_rwkv3 = nc - d_attn3 - LORA_PAD
    seq_tiles = seq // tm
    d4, d16 = DILATIONS[1], DILATIONS[2]
    assert tm == BAND * d4 and d16 == d4 * d4 and seq % (BAND * d16) == 0
    sub16 = BAND * d16 // tm
    kern = functools.partial(_inproj_kernel, seq_tiles=seq_tiles, n_pairs=n_pairs, d_rwkv3=d_rwkv3)
    return pl.pallas_call(
        kern,
        grid=(t // tm,),
        in_specs=[
            pl.BlockSpec((tm, d), lambda i: (i, 0)),
            pl.BlockSpec((1, d), lambda i: (0, 0)),
            pl.BlockSpec((d, nc), lambda i: (0, 0)),
            pl.BlockSpec((1, nc - d_attn3), lambda i: (0, 0)),
        ],
        out_specs=[
            pl.BlockSpec((n_pairs, None, tm // BAND, 1, BAND, QKV_W),
                         lambda i: (0, i // seq_tiles, i % seq_tiles, 0, 0, 0)),
            pl.BlockSpec((n_pairs, None, None, d4, BAND, QKV_W),
                         lambda i: (0, i // seq_tiles, i % seq_tiles, 0, 0, 0)),
            pl.BlockSpec((n_pairs, None, None, d16, tm // d16, QKV_W),
                         lambda i: (0, i // seq_tiles, (i % seq_tiles) // sub16, 0,
                                    (i % seq_tiles) % sub16, 0)),
            pl.BlockSpec((tm, d_rwkv3), lambda i: (i, 0)),
            pl.BlockSpec((tm, LORA_PAD), lambda i: (i, 0)),
        ],
        out_shape=[
            jax.ShapeDtypeStruct((n_pairs, batch, seq // BAND, 1, BAND, QKV_W), BF16),
            jax.ShapeDtypeStruct((n_pairs, batch, seq // (BAND * d4), d4, BAND, QKV_W), BF16),
            jax.ShapeDtypeStruct((n_pairs, batch, seq // (BAND * d16), d16, BAND, QKV_W), BF16),
            jax.ShapeDtypeStruct((t, d_rwkv3), F32),
            jax.ShapeDtypeStruct((t, LORA_PAD), F32),
        ],
        scratch_shapes=[pltpu.VMEM((1, nc - d_attn3), F32),
                        pltpu.VMEM((n_pairs, QKV_W // LANES, tm, LANES), F32),
                        pltpu.VMEM((n_pairs, QKV_W // LANES, d4, tm // d4, LANES), F32)],
        compiler_params=pltpu.CompilerParams(
            dimension_semantics=("arbitrary",), vmem_limit_bytes=VMEM_LIMIT),
        name="inproj",
    )(x2, g, w, mu)


def _attn_scores(x_ref, probs, head0):
    scale = 1.0 / math.sqrt(HEAD_DIM)
    out = []
    for r, n in probs:
        prev = jnp.maximum(n - 1, 0)
        q = x_ref[n, r, :, 0:LANES] * scale
        kw = jnp.concatenate([x_ref[prev, r, :, LANES:2 * LANES], x_ref[n, r, :, LANES:2 * LANES]],
                             axis=0)
        out.append(_nt(_block_diag(q, head0), kw))
    return out


def _attn_softmax(probs, scores, consts):
    head0, bias_first, bias_rest = consts
    s = []
    for (r, n), si in zip(probs, scores):
        bias = jnp.where(n == 0, bias_first, bias_rest)
        s.append(jnp.concatenate([si[:BAND] + bias, si[BAND:] + bias], axis=0))
    m = [jnp.max(si, axis=-1, keepdims=True) for si in s]
    p = [jnp.exp(si - mi).astype(BF16) for si, mi in zip(s, m)]
    return p, [jnp.where(head0, mi[:BAND], mi[BAND:]) for mi in m]


def _attn_pv(x_ref, probs, p, head0):
    ones = jnp.ones((2 * BAND, LANES), BF16)
    pv = []
    for (r, n), pi in zip(probs, p):
        prev = jnp.maximum(n - 1, 0)
        vw = jnp.concatenate([x_ref[prev, r, :, 2 * LANES:], x_ref[n, r, :, 2 * LANES:]], axis=0)
        pv.append(_mm(pi, jnp.concatenate([vw, ones], axis=1)))
    return [tuple(jnp.where(head0, t[:BAND], t[BAND:]) for t in (pvi[:, :LANES], pvi[:, LANES:]))
            for pvi in pv]


def _attn_kernel(x1_ref, x4_ref, x16_ref, out_ref, nat_s, ph_s, mid_s, p_scr):
    seq = out_ref.shape[0]
    lane = lax.broadcasted_iota(jnp.int32, (1, LANES), 1)
    head0 = lane < HEAD_DIM
    qrow = lax.broadcasted_iota(jnp.int32, (BAND, 1), 0)
    kcol = lax.broadcasted_iota(jnp.int32, (1, 2 * BAND), 1)
    rel = BAND + qrow - kcol
    bias_rest = jnp.where((rel >= 0) & (rel <= BAND), 0.0, NEG_BIG).astype(F32)
    bias_first = jnp.where(kcol < BAND, NEG_BIG, bias_rest)
    consts = (head0, bias_first, bias_rest)
    group = ATTN_GROUP
    n_groups = seq // (BAND * group)

    for bi, (d, x_ref) in reversed(list(enumerate(zip(DILATIONS, (x1_ref, x4_ref, x16_ref))))):
        n_blocks = seq // (BAND * d)
        shift = n_blocks.bit_length() - 1
        sub_len = seq // d
        dst = nat_s.at[bi] if d == 1 else ph_s

        def probs_of(g, shift=shift, n_blocks=n_blocks):
            js = [g * group + i for i in range(group)]
            return [(j >> shift, j & (n_blocks - 1)) for j in js], js

        def store(js, q, vals, dst=dst):
            for j, t in zip(js, vals):
                dst[q, pl.ds(pl.multiple_of(j * BAND, BAND), BAND), :] = t

        probs0, js0 = probs_of(jnp.int32(0))
        p0, m0 = _attn_softmax(probs0, _attn_scores(x_ref, probs0, head0), consts)
        store(js0, 1, m0)
        for i, pi in enumerate(p0):
            p_scr[i] = pi

        def body(g, carry, x_ref=x_ref, probs_of=probs_of, store=store):
            nxt, js_next = probs_of(jnp.minimum(g + 1, n_groups - 1))
            s_next = _attn_scores(x_ref, nxt, head0)
            cur, js = probs_of(g)
            acc_l = _attn_pv(x_ref, cur, [p_scr[i] for i in range(group)], head0)
            p_next, m_next = _attn_softmax(nxt, s_next, consts)
            store(js, 0, [t[0] for t in acc_l])
            store(js, 2, [t[1] for t in acc_l])
            store(js_next, 1, m_next)
            for i, pi in enumerate(p_next):
                p_scr[i] = pi
            return carry

        lax.fori_loop(0, n_groups, body, 0, unroll=True)
        f = DILATIONS[1]
        if d == f:
            for r in range(d):
                staged = slice(r * sub_len, (r + 1) * sub_len)
                for q in range(3):
                    nat_s[bi, q, pl.ds(r, sub_len, stride=f), :] = ph_s[q, staged, :]
        elif d == f * f:
            for q in range(3):
                for b in range(f):
                    for a in range(f):
                        staged = slice((f * a + b) * sub_len, (f * a + b + 1) * sub_len)
                        mid_s[pl.ds(b * f * sub_len + a, sub_len, stride=f), :] = ph_s[q, staged, :]
                for b in range(f):
                    rows = slice(b * f * sub_len, (b + 1) * f * sub_len)
                    nat_s[bi, q, pl.ds(b, f * sub_len, stride=f), :] = mid_s[rows, :]

    rows_c = 512
    for c in range(seq // rows_c):
        rows = slice(c * rows_c, (c + 1) * rows_c)
        ms = [nat_s[bi, 1, rows, :] for bi in range(3)]
        top = jnp.maximum(jnp.maximum(ms[0], ms[1]), ms[2])
        es = [jnp.exp(mb - top) for mb in ms]
        num = sum(e * nat_s[bi, 0, rows, :] for bi, e in enumerate(es))
        den = sum(e * nat_s[bi, 2, rows, :] for bi, e in enumerate(es))
        out_ref[rows, :] = (num / den).astype(out_ref.dtype)


def _attention(x1, x4, x16, *, batch, seq):
    n_pairs = x1.shape[0]
    spec = lambda x: pl.BlockSpec((None, None) + x.shape[2:], lambda b, p: (p, b, 0, 0, 0, 0))
    return pl.pallas_call(
        _attn_kernel,
        grid=(batch, n_pairs),
        in_specs=[spec(x1), spec(x4), spec(x16)],
        out_specs=pl.BlockSpec((None, seq, LANES), lambda b, p: (p, b, 0)),
        out_shape=jax.ShapeDtypeStruct((n_pairs, batch * seq, LANES), BF16),
        scratch_shapes=[pltpu.VMEM((3, 3, seq, LANES), F32), pltpu.VMEM((3, seq, LANES), F32),
                        pltpu.VMEM((seq, LANES), F32),
                        pltpu.VMEM((ATTN_GROUP, 2 * BAND, 2 * BAND), BF16)],
        compiler_params=pltpu.CompilerParams(
            dimension_semantics=("arbitrary", "arbitrary"), vmem_limit_bytes=VMEM_LIMIT),
        name="attention",
    )(x1, x4, x16)


def _block_diag(x, head0):
    zero = jnp.zeros_like(x)
    return jnp.concatenate([jnp.where(head0, x, zero), jnp.where(head0, zero, x)], axis=0)


def _chunk_consts():
    c = CHUNK
    trow = lax.broadcasted_iota(jnp.int32, (c, 1), 0)
    col3 = lax.broadcasted_iota(jnp.int32, (1, 3 * c), 1)
    tri3 = jnp.where((col3 & (c - 1)) <= trow, 1.0, 0.0).astype(BF16)
    lane_w = lax.broadcasted_iota(jnp.int32, (1, LANES), 1)
    scol = lane_w & (HEAD_DIM - 1)
    strict = scol < trow
    incl = scol <= trow
    eye_w = jnp.where(scol == trow, 1.0, 0.0).astype(F32)
    head0 = lane_w < HEAD_DIM
    rr = lax.broadcasted_iota(jnp.int32, (LANES, 1), 0)
    same_head = (rr // HEAD_DIM) == (lane_w // HEAD_DIM)
    levels = []
    s = 1
    while s < c:
        in_pair = (trow // (2 * s)) == (scol // (2 * s))
        levels.append(in_pair & ((trow // s) != (scol // s)) & strict)
        s *= 2
    return dict(tri3=tri3, strict=strict, incl=incl, eye_w=eye_w, head0=head0,
                same_head=same_head, levels=levels)


def _chunk_precompute(ins, consts):
    c = CHUNK
    head0 = consts["head0"]
    strict, incl, same_head = consts["strict"], consts["incl"], consts["same_head"]
    levels = consts["levels"]
    zero = jnp.zeros((c, LANES), F32)
    zero_sq = jnp.zeros((LANES, LANES), F32)

    def bd(x):
        return _block_diag(x, head0).astype(BF16)

    splits = []
    for (_, lw, _, _, _, _) in ins:
        hi = lw.astype(BF16)
        r1 = lw - hi.astype(F32)
        mid = r1.astype(BF16)
        lo = (r1 - mid.astype(F32)).astype(BF16)
        splits.append(jnp.concatenate([hi, mid, lo], axis=0))
    cums = []
    for i in range(0, len(splits), 2):
        both = _mm(consts["tri3"], jnp.concatenate(splits[i:i + 2], axis=1))
        cums += [both[:, :LANES], both[:, LANES:]]
    yield
    ops = []
    for (r, lw, k, v, kk, b), cum in zip(ins, cums):
        cum_end = cum[c - 1:c, :]
        e_neg = jnp.exp(-cum)
        e_end = jnp.exp(cum_end - cum)
        ops.append(dict(kq=kk * jnp.exp(cum - lw), rq=r * jnp.exp(cum), kt=k * e_neg, bt=b * e_neg,
                        kh=(k * e_end).astype(BF16), bh=(b * e_end).astype(BF16),
                        g=jnp.exp(cum_end), v=v))
    a_all = [_nt(jnp.concatenate([o["kq"], o["rq"]], axis=0).astype(BF16),
                 jnp.concatenate([bd(o["bt"]), bd(o["kt"])], axis=0)) for o in ops]
    a_ab = [jnp.where(strict, a[:c, :LANES], zero) for a in a_all]
    a_rb = [jnp.where(incl, a[c:, :LANES], zero).astype(BF16) for a in a_all]
    av = [_mm(jnp.concatenate([jnp.where(strict, a[:c, LANES:], zero),
                               jnp.where(incl, a[c:, LANES:], zero)], axis=0).astype(BF16),
              bd(o["v"])) for a, o in zip(a_all, ops)]
    yield
    t_inv = [consts["eye_w"] - jnp.where(levels[0], a, zero) for a in a_ab]
    for mask in levels[1:]:
        x = [_mm(jnp.where(mask, a, zero).astype(BF16), bd(t)) for a, t in zip(a_ab, t_inv)]
        yield
        t_inv = [t - _mm(t.astype(BF16), bd(xi)) for t, xi in zip(t_inv, x)]
        yield
    twu = [_mm(t.astype(BF16), jnp.concatenate([bd(o["kq"]), bd(a[:c])], axis=1))
           for t, o, a in zip(t_inv, ops, av)]
    yield
    arb = [_mm(ar, jnp.concatenate([bd(x[:, :LANES]), bd(x[:, LANES:])], axis=1))
           for ar, x in zip(a_rb, twu)]
    yield
    out = []
    for o, a, x, y in zip(ops, av, twu, arb):
        w = x[:, :LANES].astype(BF16)
        u0 = x[:, LANES:]
        pm = -jnp.where(same_head, _tn(w, o["bh"]), zero_sq)
        dd = jnp.where(same_head,
                       _tn(jnp.concatenate([o["v"], -u0], axis=0).astype(BF16),
                           jnp.concatenate([o["kh"], o["bh"]], axis=0)), zero_sq)
        out.append((o["rq"] - y[:, :LANES], a[c:] - y[:, LANES:], pm, dd, o["g"]))
    return out


def _spread(fillers, every=2):
    out = []
    for f in fillers:
        out.append(f)
        out.extend([lambda: None] * (every - 1))
    return out


def _interleave(gen, fillers):
    fillers = list(fillers)
    while True:
        try:
            next(gen)
        except StopIteration as stop:
            result = stop.value
            break
        if fillers:
            fillers.pop(0)()
    for f in fillers:
        f()
    return result


def _chunk_apply(pre, states):
    sb = [s.astype(BF16) for s in states]
    ys = [_nt(p[0].astype(BF16), s) + p[1] for p, s in zip(pre, sb)]
    new = [st * p[4] + _mm(s, p[2].astype(BF16)) + p[3] for p, s, st in zip(pre, sb, states)]
    return ys, new


def _rwkv_kernel(*refs, first_layer, d_rwkv):
    if first_layer:
        (zr_ref, zl_ref, wl_ref, vec_ref, out_ref, vfirst_out_ref,
         r_s, lw_s, k_s, v_s, kk_s, b_s, g_s, y_s, state_s) = refs
        vfirst_ref = None
    else:
        (zr_ref, zl_ref, vfirst_ref, wl_ref, vec_ref, out_ref,
         r_s, lw_s, k_s, v_s, kk_s, b_s, g_s, y_s, state_s) = refs
    tc = zr_ref.shape[0]
    n_pairs = d_rwkv // LANES
    c = CHUNK
    sub = RWKV_SUB
    sub_chunks = sub // c

    @pl.when(pl.program_id(1) == 0)
    def _():
        state_s[...] = jnp.zeros_like(state_s)

    w0, a0, mv0, k_k, k_a, r_k, gn_w, gn_b = (vec_ref[j:j + 1, :] for j in range(8))
    hrow = lax.broadcasted_iota(jnp.int32, (2 * LANES, 1), 0)
    hcol = lax.broadcasted_iota(jnp.int32, (1, 2 * LANES), 1)
    same = jnp.where((hrow // HEAD_DIM) == (hcol // HEAD_DIM), 1.0, 0.0).astype(BF16)

    def head_sum(x):
        xb = x.astype(BF16)
        w = 2 * LANES
        return jnp.concatenate([_mm(xb[:, j * w:(j + 1) * w], same) for j in range(d_rwkv // w)],
                               axis=1)

    lane = lax.broadcasted_iota(jnp.int32, (1, LORA_PAD), 1)
    o1 = LORA_DECAY
    o2 = o1 + LORA_AAA
    o3 = o2 + LORA_GATE

    def prepare(rows):
        zl = zl_ref[rows, :]
        act = jnp.where(lane < o1, jnp.tanh(zl),
                        jnp.where((lane >= o2) & (lane < o3), jax.nn.sigmoid(zl), zl))
        lora = _mm(act.astype(BF16), wl_ref[...])
        r = zr_ref[rows, 0:d_rwkv]
        k = zr_ref[rows, d_rwkv:2 * d_rwkv]
        v = zr_ref[rows, 2 * d_rwkv:3 * d_rwkv]
        lw = -math.exp(-0.5) * jax.nn.sigmoid(w0 + lora[:, 0:d_rwkv])
        a = jax.nn.sigmoid(a0 + lora[:, d_rwkv:2 * d_rwkv])
        if first_layer:
            vfirst_out_ref[rows, :] = v
        else:
            vgate = jax.nn.sigmoid(mv0 + lora[:, 3 * d_rwkv:4 * d_rwkv])
            v = v + (vfirst_ref[rows, :] - v) * vgate
        kk = k * k_k
        kk = kk / jnp.maximum(jnp.sqrt(head_sum(kk * kk)), 1e-12)
        r_s[rows, :] = r
        lw_s[rows, :] = lw
        k_s[rows, :] = k * (1.0 + (a - 1.0) * k_a)
        v_s[rows, :] = v
        kk_s[rows, :] = kk
        b_s[rows, :] = kk * a
        g_s[rows, :] = lora[:, 2 * d_rwkv:3 * d_rwkv]

    def problems(h):
        ins = []
        for ci in range(h * sub_chunks, (h + 1) * sub_chunks):
            rows = slice(ci * c, (ci + 1) * c)
            for p in range(n_pairs):
                ls = slice(p * LANES, (p + 1) * LANES)
                ins.append(tuple(ref[rows, ls] for ref in (r_s, lw_s, k_s, v_s, kk_s, b_s)))
        return ins

    states = [state_s[p] for p in range(n_pairs)]

    def chain_steps(h, pre):
        def step(j):
            def run():
                ys, new = _chunk_apply(pre[j * n_pairs:(j + 1) * n_pairs], states)
                states[:] = new
                ci = h * sub_chunks + j
                for p in range(n_pairs):
                    y_s[ci * c:(ci + 1) * c, p * LANES:(p + 1) * LANES] = ys[p]
            return run

        rows = slice(h * sub, (h + 1) * sub)
        tmp = {}

        def stats():
            y = y_s[rows, :]
            tmp["dlt"] = y - head_sum(y) * (1.0 / HEAD_DIM)
            tmp["bonus"] = head_sum(r_s[rows, :] * k_s[rows, :] * r_k) * v_s[rows, :]

        def finish():
            dlt = tmp["dlt"]
            var = head_sum(dlt * dlt) * (1.0 / HEAD_DIM)
            yn = dlt * lax.rsqrt(var + GN_EPS) * gn_w + gn_b
            out_ref[rows, :] = ((yn + tmp["bonus"]) * g_s[rows, :]).astype(out_ref.dtype)

        return [step(j) for j in range(sub_chunks)] + [stats, finish]

    consts = _chunk_consts()
    n_sub = tc // sub
    for h in range(n_sub):
        prepare(slice(h * sub, (h + 1) * sub))
    pending = []
    for h in range(n_sub):
        pre = _interleave(_chunk_precompute(problems(h), consts), _spread(pending))
        pending = chain_steps(h, pre)
    for f in pending:
        f()
    for p in range(n_pairs):
        state_s[p] = states[p]


def _rwkv(zr, zl, vfirst, wl, vecs, *, batch, seq, tc=RWKV_TC):
    t = zr.shape[0]
    d_rwkv = zr.shape[1] // 3
    first_layer = vfirst is None
    tiles = seq // tc
    row = lambda b, i: (b * tiles + i, 0)
    const = lambda b, i: (0, 0)
    in_specs = [pl.BlockSpec((tc, 3 * d_rwkv), row), pl.BlockSpec((tc, LORA_PAD), row)]
    args = [zr, zl]
    if not first_layer:
        in_specs.append(pl.BlockSpec((tc, d_rwkv), row))
        args.append(vfirst)
    in_specs += [pl.BlockSpec(wl.shape, const), pl.BlockSpec(vecs.shape, const)]
    args += [wl, vecs]
    out_specs = [pl.BlockSpec((tc, d_rwkv), row)]
    out_shape = [jax.ShapeDtypeStruct((t, d_rwkv), BF16)]
    if first_layer:
        out_specs.append(pl.BlockSpec((tc, d_rwkv), row))
        out_shape.append(jax.ShapeDtypeStruct((t, d_rwkv), F32))
    kern = functools.partial(_rwkv_kernel, first_layer=first_layer, d_rwkv=d_rwkv)
    res = pl.pallas_call(
        kern,
        grid=(batch, tiles),
        in_specs=in_specs,
        out_specs=out_specs,
        out_shape=out_shape,
        scratch_shapes=[pltpu.VMEM((tc, d_rwkv), F32) for _ in range(8)]
        + [pltpu.VMEM((d_rwkv // LANES, LANES, LANES), F32)],
        compiler_params=pltpu.CompilerParams(
            dimension_semantics=("arbitrary", "arbitrary"), vmem_limit_bytes=VMEM_LIMIT),
        name="rwkv7_first" if first_layer else "rwkv7",
    )(*args)
    return (res[0], res[1]) if first_layer else (res[0], vfirst)


def _mix_ffn_kernel(a_ref, rw_ref, x_ref, gains_ref, ga_ref, wo_ref, wu_ref, wd_ref, out_ref,
                    acc_ref, *, ff_chunk):
    n_pairs = a_ref.shape[0]
    tm = x_ref.shape[0]
    g_post, g_pre, g_ffn = (gains_ref[j:j + 1, :] for j in range(3))
    d_attn = n_pairs * LANES
    d_ff = wu_ref.shape[1]
    halves = [slice(j * (tm // 2), (j + 1) * (tm // 2)) for j in range(2)]
    x1s, hs = [], []
    for rows in halves:
        attn = jnp.concatenate([a_ref[p, rows, :].astype(F32) for p in range(n_pairs)], axis=-1)
        attn = _rms(attn, ga_ref[...]).astype(BF16)
        mixed = _mm(attn, wo_ref[0:d_attn, :]) + _mm(rw_ref[rows, :], wo_ref[d_attn:, :])
        x1 = x_ref[rows, :] + _rms(mixed, g_post)
        x1s.append(x1)
        hs.append(_rms(x1, g_pre).astype(BF16))
    for rows, h in zip(halves, hs):
        for c in range(d_ff // ff_chunk):
            cs = slice(c * ff_chunk, (c + 1) * ff_chunk)
            u = jnp.maximum(_mm(h, wu_ref[:, cs]), 0.0)
            part = _mm((u * u).astype(BF16), wd_ref[cs, :])
            if c == 0:
                acc_ref[rows, :] = part
            else:
                acc_ref[rows, :] += part
    for rows, x1 in zip(halves, x1s):
        out_ref[rows, :] = x1 + _rms(acc_ref[rows, :], g_ffn)


def _mix_ffn(attn, rw, x2, gains, ga, wo, wu, wd, *, layer, tm=512, ff_chunk=1024):
    t, d = x2.shape
    n_pairs = attn.shape[0]
    d_rwkv = rw.shape[1]
    row = lambda i: (i, 0)
    const = lambda i: (0, 0)
    resident = lambda shape: pl.BlockSpec((None,) + shape[1:], lambda i: (layer, 0, 0),
                                          pipeline_mode=pl.Buffered(1))
    return pl.pallas_call(
        functools.partial(_mix_ffn_kernel, ff_chunk=ff_chunk),
        grid=(t // tm,),
        in_specs=[pl.BlockSpec((n_pairs, tm, LANES), lambda i: (0, i, 0)),
                  pl.BlockSpec((tm, d_rwkv), row), pl.BlockSpec((tm, d), row),
                  pl.BlockSpec(gains.shape, const), pl.BlockSpec(ga.shape, const),
                  resident(wo.shape), resident(wu.shape), resident(wd.shape)],
        out_specs=pl.BlockSpec((tm, d), row),
        out_shape=jax.ShapeDtypeStruct((t, d), F32),
        scratch_shapes=[pltpu.VMEM((tm, d), F32)],
        compiler_params=pltpu.CompilerParams(
            dimension_semantics=("arbitrary",), vmem_limit_bytes=VMEM_LIMIT),
        name="mix_ffn",
    )(attn, rw, x2, gains, ga, wo, wu, wd)


def _layer_params(i, d_attn, d_rwkv, w_in_first, w_in_rest, mu_shift, mu_shift_mv,
                  decay_up, aaa_up, gate_up, mv_up):
    w = w_in_first if i == 0 else w_in_rest[i - 1]
    d = w.shape[0]
    n_first = w_in_first.shape[1]
    lora_used = LORA_DECAY + LORA_AAA + LORA_GATE + LORA_MV
    n_pairs = d_attn // LANES
    qkv = [w[:, j * d_attn + p * LANES:j * d_attn + (p + 1) * LANES]
           for p in range(n_pairs) for j in range(3)]
    pieces = qkv + [w[:, 3 * d_attn:n_first]]
    mus = [mu_shift[i]]
    if i == 0:
        pieces.append(jnp.zeros((d, LORA_MV), F32))
        mus.append(jnp.zeros((LORA_MV,), F32))
    else:
        pieces.append(w[:, n_first:])
        mus.append(mu_shift_mv[i - 1])
    pieces.append(jnp.zeros((d, LORA_PAD - lora_used), F32))
    mus.append(jnp.zeros((LORA_PAD - lora_used,), F32))
    w_r = jnp.concatenate(pieces, axis=1).astype(BF16)
    mu_r = jnp.concatenate(mus)[None, :]
    wl = jnp.zeros((LORA_PAD, 4 * d_rwkv), F32)
    o1 = LORA_DECAY
    o2 = o1 + LORA_AAA
    o3 = o2 + LORA_GATE
    wl = wl.at[0:o1, 0:d_rwkv].set(decay_up[i])
    wl = wl.at[o1:o2, d_rwkv:2 * d_rwkv].set(aaa_up[i])
    wl = wl.at[o2:o3, 2 * d_rwkv:3 * d_rwkv].set(gate_up[i])
    if i > 0:
        wl = wl.at[o3:o3 + LORA_MV, 3 * d_rwkv:4 * d_rwkv].set(mv_up[i - 1])
    return w_r, mu_r, wl.astype(BF16)


def kernel(x, norm_mix_pre, norm_mix_post, norm_ffn_pre, norm_ffn_post, w_in_first, w_in_rest,
           mu_shift, mu_shift_mv, attn_out_gain, decay_w0, decay_up, aaa_a0, aaa_up, mv_v0, mv_up,
           gate_up, k_k, k_a, r_k, gn_w, gn_b, w_out, w_ffn_up, w_ffn_down):
    batch, seq, d = x.shape
    depth = norm_mix_pre.shape[0]
    d_attn = attn_out_gain.shape[1]
    d_rwkv = decay_w0.shape[1]
    x2 = x.reshape(batch * seq, d)
    wo_all, wu_all, wd_all = (w.astype(BF16) for w in (w_out, w_ffn_up, w_ffn_down))
    vfirst = None
    for i in range(depth):
        w_r, mu_r, wl = _layer_params(i, d_attn, d_rwkv, w_in_first, w_in_rest, mu_shift,
                                      mu_shift_mv, decay_up, aaa_up, gate_up, mv_up)
        mv0 = mv_v0[i - 1] if i > 0 else jnp.zeros((d_rwkv,), F32)
        vecs = jnp.stack([decay_w0[i], aaa_a0[i], mv0, k_k[i], k_a[i], r_k[i].reshape(-1),
                          gn_w[i], gn_b[i]])
        x1, x4, x16, zr, zl = _inproj(x2, norm_mix_pre[i][None, :], w_r, mu_r, batch=batch, seq=seq)
        attn = _attention(x1, x4, x16, batch=batch, seq=seq)
        rw, vfirst = _rwkv(zr, zl, vfirst, wl, vecs, batch=batch, seq=seq)
        gains = jnp.stack([norm_mix_post[i], norm_ffn_pre[i], norm_ffn_post[i]])
        x2 = _mix_ffn(attn, rw, x2, gains, attn_out_gain[i][None, :], wo_all, wu_all, wd_all,
                      layer=i)
    return x2.reshape(batch, seq, d)
```

```python
import functools
import math

import jax
import jax.numpy as jnp
from jax import lax
from jax.experimental import pallas as pl
from jax.experimental.pallas import tpu as pltpu

F32 = jnp.float32
BF16 = jnp.bfloat16

HEAD_DIM = 64
LANES = 128
NORM_EPS = 1e-6
GN_EPS = 64e-5
DILATIONS = (1, 4, 16)
BAND = 128
QKV_W = 3 * LANES
INPROJ_TM = BAND * DILATIONS[1]
ATTN_GROUP = 1
LORA_DECAY, LORA_AAA, LORA_GATE, LORA_MV = 32, 32, 96, 32
LORA_PAD = 256
MIX_TM = 1024
MIX_SUB = 256
CHUNK = 64
RWKV_TC = 1024
RWKV_SUB = 256
NEG_BIG = -1e30
VMEM_LIMIT = 56 * 1024 * 1024


def _nt(a, b):
    return lax.dot_general(a, b, (((1,), (1,)), ((), ())), preferred_element_type=F32)


def _tn(a, b):
    return lax.dot_general(a, b, (((0,), (0,)), ((), ())), preferred_element_type=F32)


def _mm(a, b):
    return jnp.dot(a, b, preferred_element_type=F32)


def _rms(x, g):
    return x * lax.rsqrt(jnp.mean(x * x, axis=-1, keepdims=True) + NORM_EPS) * g


def _inproj_kernel(x_ref, g_ref, w_ref, mu_ref, x1_ref, x4_ref, x16_ref, zr_ref, zl_ref,
                   carry_ref, zs_ref, z4_ref, *, seq_tiles, n_pairs, d_rwkv3):
    i = pl.program_id(0)
    tm = x_ref.shape[0]
    h = _rms(x_ref[...], g_ref[...]).astype(BF16)
    d_attn3 = n_pairs * QKV_W
    f = DILATIONS[1]

    def attn_pairs(p0):
        zz = _mm(h, w_ref[:, p0 * QKV_W:(p0 + 2) * QKV_W])
        for p in (p0, p0 + 1):
            z = zz[:, (p - p0) * QKV_W:(p - p0 + 1) * QKV_W]
            x1_ref[p, :, 0] = z.astype(BF16).reshape(tm // BAND, BAND, QKV_W)
            for j in range(QKV_W // LANES):
                ls = slice(j * LANES, (j + 1) * LANES)
                zs_ref[p, j] = z[:, ls]
                for b in range(f):
                    z4 = zs_ref[p, j, pl.ds(b, tm // f, stride=f), :]
                    x4_ref[p, b, :, ls] = z4.astype(BF16)
                    z4_ref[p, j, b] = z4
                    for a in range(f):
                        z16 = z4_ref[p, j, b, pl.ds(a, tm // (f * f), stride=f), :]
                        x16_ref[p, f * a + b, :, ls] = z16.astype(BF16)

    first_tile = (i % seq_tiles) == 0
    row0 = lax.broadcasted_iota(jnp.int32, (tm, 1), 0) == 0

    def shifted(c0, w):
        z = _mm(h, w_ref[:, d_attn3 + c0:d_attn3 + c0 + w])
        carry = jnp.where(first_tile, 0.0, carry_ref[:, c0:c0 + w])
        prev = jnp.where(row0, carry, pltpu.roll(z, 1, 0))
        carry_ref[:, c0:c0 + w] = z[tm - 1:tm, :]
        zs = z + (prev - z) * mu_ref[:, c0:c0 + w]
        if c0 < d_rwkv3:
            zr_ref[:, c0:c0 + w] = zs
        else:
            zl_ref[...] = zs

    n_shift = d_rwkv3 + LORA_PAD
    cw = 512
    shift_tasks = [(c0, min(cw, n_shift - c0)) for c0 in range(0, n_shift, cw)]
    attn_tasks = list(range(0, n_pairs, 2))
    while attn_tasks or shift_tasks:
        if attn_tasks:
            attn_pairs(attn_tasks.pop(0))
        for _ in range(2):
            if shift_tasks:
                shifted(*shift_tasks.pop(0))


def _inproj(x2, g, w, mu, *, batch, seq):
    tm = INPROJ_TM
    t, d = x2.shape
    nc = w.shape[1]
    n_pairs = (d // 2) // LANES
    d_attn3 = n_pairs * QKV_W
    d_rwkv3 = nc - d_attn3 - LORA_PAD
    seq_tiles = seq // tm
    d4, d16 = DILATIONS[1], DILATIONS[2]
    assert tm == BAND * d4 and d16 == d4 * d4 and seq % (BAND * d16) == 0
    sub16 = BAND * d16 // tm
    kern = functools.partial(_inproj_kernel, seq_tiles=seq_tiles, n_pairs=n_pairs, d_rwkv3=d_rwkv3)
    return pl.pallas_call(
        kern,
        grid=(t // tm,),
        in_specs=[
            pl.BlockSpec((tm, d), lambda i: (i, 0)),
            pl.BlockSpec((1, d), lambda i: (0, 0)),
            pl.BlockSpec((d, nc), lambda i: (0, 0)),
            pl.BlockSpec((1, nc - d_attn3), lambda i: (0, 0)),
        ],
        out_specs=[
            pl.BlockSpec((n_pairs, None, tm // BAND, 1, BAND, QKV_W),
                         lambda i: (0, i // seq_tiles, i % seq_tiles, 0, 0, 0)),
            pl.BlockSpec((n_pairs, None, None, d4, BAND, QKV_W),
                         lambda i: (0, i // seq_tiles, i % seq_tiles, 0, 0, 0)),
            pl.BlockSpec((n_pairs, None, None, d16, tm // d16, QKV_W),
                         lambda i: (0, i // seq_tiles, (i % seq_tiles) // sub16, 0,
                                    (i % seq_tiles) % sub16, 0)),
            pl.BlockSpec((tm, d_rwkv3), lambda i: (i, 0)),
            pl.BlockSpec((tm, LORA_PAD), lambda i: (i, 0)),
        ],
        out_shape=[
            jax.ShapeDtypeStruct((n_pairs, batch, seq // BAND, 1, BAND, QKV_W), BF16),
            jax.ShapeDtypeStruct((n_pairs, batch, seq // (BAND * d4), d4, BAND, QKV_W), BF16),
            jax.ShapeDtypeStruct((n_pairs, batch, seq // (BAND * d16), d16, BAND, QKV_W), BF16),
            jax.ShapeDtypeStruct((t, d_rwkv3), F32),
            jax.ShapeDtypeStruct((t, LORA_PAD), F32),
        ],
        scratch_shapes=[pltpu.VMEM((1, nc - d_attn3), F32),
                        pltpu.VMEM((n_pairs, QKV_W // LANES, tm, LANES), F32),
                        pltpu.VMEM((n_pairs, QKV_W // LANES, d4, tm // d4, LANES), F32)],
        compiler_params=pltpu.CompilerParams(
            dimension_semantics=("arbitrary",), vmem_limit_bytes=VMEM_LIMIT),
        name="inproj",
    )(x2, g, w, mu)


def _attn_scores(x_ref, probs, head0):
    scale = 1.0 / math.sqrt(HEAD_DIM)
    out = []
    for r, n in probs:
        prev = jnp.maximum(n - 1, 0)
        q = x_ref[n, r, :, 0:LANES] * scale
        kw = jnp.concatenate([x_ref[prev, r, :, LANES:2 * LANES], x_ref[n, r, :, LANES:2 * LANES]],
                             axis=0)
        out.append(_nt(_block_diag(q, head0), kw))
    return out


def _attn_softmax(probs, scores, consts):
    head0, bias_first, bias_rest = consts
    s = []
    for (r, n), si in zip(probs, scores):
        bias = jnp.where(n == 0, bias_first, bias_rest)
        s.append(jnp.concatenate([si[:BAND] + bias, si[BAND:] + bias], axis=0))
    m = [jnp.max(si, axis=-1, keepdims=True) for si in s]
    p = [jnp.exp(si - mi).astype(BF16) for si, mi in zip(s, m)]
    return p, [jnp.where(head0, mi[:BAND], mi[BAND:]) for mi in m]


def _attn_pv(x_ref, probs, p, head0):
    ones = jnp.ones((2 * BAND, LANES), BF16)
    pv = []
    for (r, n), pi in zip(probs, p):
        prev = jnp.maximum(n - 1, 0)
        vw = jnp.concatenate([x_ref[prev, r, :, 2 * LANES:], x_ref[n, r, :, 2 * LANES:]], axis=0)
        pv.append(_mm(pi, jnp.concatenate([vw, ones], axis=1)))
    return [tuple(jnp.where(head0, t[:BAND], t[BAND:]) for t in (pvi[:, :LANES], pvi[:, LANES:]))
            for pvi in pv]


def _attn_kernel(x1_ref, x4_ref, x16_ref, out_ref, nat_s, ph_s, mid_s, p_scr):
    seq = out_ref.shape[0]
    lane = lax.broadcasted_iota(jnp.int32, (1, LANES), 1)
    head0 = lane < HEAD_DIM
    qrow = lax.broadcasted_iota(jnp.int32, (BAND, 1), 0)
    kcol = lax.broadcasted_iota(jnp.int32, (1, 2 * BAND), 1)
    rel = BAND + qrow - kcol
    bias_rest = jnp.where((rel >= 0) & (rel <= BAND), 0.0, NEG_BIG).astype(F32)
    bias_first = jnp.where(kcol < BAND, NEG_BIG, bias_rest)
    consts = (head0, bias_first, bias_rest)
    group = ATTN_GROUP
    n_groups = seq // (BAND * group)

    for bi, (d, x_ref) in reversed(list(enumerate(zip(DILATIONS, (x1_ref, x4_ref, x16_ref))))):
        n_blocks = seq // (BAND * d)
        shift = n_blocks.bit_length() - 1
        sub_len = seq // d
        dst = nat_s.at[bi] if d == 1 else ph_s

        def probs_of(g, shift=shift, n_blocks=n_blocks):
            js = [g * group + i for i in range(group)]
            return [(j >> shift, j & (n_blocks - 1)) for j in js], js

        def store(js, q, vals, dst=dst):
            for j, t in zip(js, vals):
                dst[q, pl.ds(pl.multiple_of(j * BAND, BAND), BAND), :] = t

        probs0, js0 = probs_of(jnp.int32(0))
        p0, m0 = _attn_softmax(probs0, _attn_scores(x_ref, probs0, head0), consts)
        store(js0, 1, m0)
        for i, pi in enumerate(p0):
            p_scr[i] = pi

        def body(g, carry, x_ref=x_ref, probs_of=probs_of, store=store):
            nxt, js_next = probs_of(jnp.minimum(g + 1, n_groups - 1))
            s_next = _attn_scores(x_ref, nxt, head0)
            cur, js = probs_of(g)
            acc_l = _attn_pv(x_ref, cur, [p_scr[i] for i in range(group)], head0)
            p_next, m_next = _attn_softmax(nxt, s_next, consts)
            store(js, 0, [t[0] for t in acc_l])
            store(js, 2, [t[1] for t in acc_l])
            store(js_next, 1, m_next)
            for i, pi in enumerate(p_next):
                p_scr[i] = pi
            return carry

        lax.fori_loop(0, n_groups, body, 0, unroll=True)
        f = DILATIONS[1]
        if d == f:
            for r in range(d):
                staged = slice(r * sub_len, (r + 1) * sub_len)
                for q in range(3):
                    nat_s[bi, q, pl.ds(r, sub_len, stride=f), :] = ph_s[q, staged, :]
        elif d == f * f:
            for q in range(3):
                for b in range(f):
                    for a in range(f):
                        staged = slice((f * a + b) * sub_len, (f * a + b + 1) * sub_len)
                        mid_s[pl.ds(b * f * sub_len + a, sub_len, stride=f), :] = ph_s[q, staged, :]
                for b in range(f):
                    rows = slice(b * f * sub_len, (b + 1) * f * sub_len)
                    nat_s[bi, q, pl.ds(b, f * sub_len, stride=f), :] = mid_s[rows, :]

    rows_c = 512
    for c in range(seq // rows_c):
        rows = slice(c * rows_c, (c + 1) * rows_c)
        ms = [nat_s[bi, 1, rows, :] for bi in range(3)]
        top = jnp.maximum(jnp.maximum(ms[0], ms[1]), ms[2])
        es = [jnp.exp(mb - top) for mb in ms]
        num = sum(e * nat_s[bi, 0, rows, :] for bi, e in enumerate(es))
        den = sum(e * nat_s[bi, 2, rows, :] for bi, e in enumerate(es))
        out_ref[rows, :] = (num / den).astype(out_ref.dtype)


def _attention(x1, x4, x16, *, batch, seq):
    n_pairs = x1.shape[0]
    spec = lambda x: pl.BlockSpec((None, None) + x.shape[2:], lambda b, p: (p, b, 0, 0, 0, 0))
    return pl.pallas_call(
        _attn_kernel,
        grid=(batch, n_pairs),
        in_specs=[spec(x1), spec(x4), spec(x16)],
        out_specs=pl.BlockSpec((None, seq, LANES), lambda b, p: (p, b, 0)),
        out_shape=jax.ShapeDtypeStruct((n_pairs, batch * seq, LANES), BF16),
        scratch_shapes=[pltpu.VMEM((3, 3, seq, LANES), F32), pltpu.VMEM((3, seq, LANES), F32),
                        pltpu.VMEM((seq, LANES), F32),
                        pltpu.VMEM((ATTN_GROUP, 2 * BAND, 2 * BAND), BF16)],
        compiler_params=pltpu.CompilerParams(
            dimension_semantics=("arbitrary", "arbitrary"), vmem_limit_bytes=VMEM_LIMIT),
        name="attention",
    )(x1, x4, x16)


def _block_diag(x, head0):
    zero = jnp.zeros_like(x)
    return jnp.concatenate([jnp.where(head0, x, zero), jnp.where(head0, zero, x)], axis=0)


def _chunk_consts():
    c = CHUNK
    trow = lax.broadcasted_iota(jnp.int32, (c, 1), 0)
    col3 = lax.broadcasted_iota(jnp.int32, (1, 3 * c), 1)
    tri3 = jnp.where((col3 & (c - 1)) <= trow, 1.0, 0.0).astype(BF16)
    lane_w = lax.broadcasted_iota(jnp.int32, (1, LANES), 1)
    scol = lane_w & (HEAD_DIM - 1)
    strict = scol < trow
    incl = scol <= trow
    eye_w = jnp.where(scol == trow, 1.0, 0.0).astype(F32)
    head0 = lane_w < HEAD_DIM
    rr = lax.broadcasted_iota(jnp.int32, (LANES, 1), 0)
    same_head = (rr // HEAD_DIM) == (lane_w // HEAD_DIM)
    levels = []
    s = 1
    while s < c:
        in_pair = (trow // (2 * s)) == (scol // (2 * s))
        levels.append(in_pair & ((trow // s) != (scol // s)) & strict)
        s *= 2
    return dict(tri3=tri3, strict=strict, incl=incl, eye_w=eye_w, head0=head0,
                same_head=same_head, levels=levels)


def _chunk_precompute(ins, consts):
    c = CHUNK
    head0 = consts["head0"]
    strict, incl, same_head = consts["strict"], consts["incl"], consts["same_head"]
    levels = consts["levels"]
    zero = jnp.zeros((c, LANES), F32)
    zero_sq = jnp.zeros((LANES, LANES), F32)

    def bd(x):
        return _block_diag(x, head0).astype(BF16)

    splits = []
    for (_, lw, _, _, _, _) in ins:
        hi = lw.astype(BF16)
        r1 = lw - hi.astype(F32)
        mid = r1.astype(BF16)
        lo = (r1 - mid.astype(F32)).astype(BF16)
        splits.append(jnp.concatenate([hi, mid, lo], axis=0))
    cums = []
    for i in range(0, len(splits), 2):
        both = _mm(consts["tri3"], jnp.concatenate(splits[i:i + 2], axis=1))
        cums += [both[:, :LANES], both[:, LANES:]]
    yield
    ops = []
    for (r, lw, k, v, kk, b), cum in zip(ins, cums):
        cum_end = cum[c - 1:c, :]
        e_neg = jnp.exp(-cum)
        e_end = jnp.exp(cum_end - cum)
        ops.append(dict(kq=kk * jnp.exp(cum - lw), rq=r * jnp.exp(cum), kt=k * e_neg, bt=b * e_neg,
                        kh=(k * e_end).astype(BF16), bh=(b * e_end).astype(BF16),
                        g=jnp.exp(cum_end), v=v))
    a_all = [_nt(jnp.concatenate([o["kq"], o["rq"]], axis=0).astype(BF16),
                 jnp.concatenate([bd(o["bt"]), bd(o["kt"])], axis=0)) for o in ops]
    a_ab = [jnp.where(strict, a[:c, :LANES], zero) for a in a_all]
    a_rb = [jnp.where(incl, a[c:, :LANES], zero).astype(BF16) for a in a_all]
    av = [_mm(jnp.concatenate([jnp.where(strict, a[:c, LANES:], zero),
                               jnp.where(incl, a[c:, LANES:], zero)], axis=0).astype(BF16),
              bd(o["v"])) for a, o in zip(a_all, ops)]
    yield
    t_inv = [consts["eye_w"] - jnp.where(levels[0], a, zero) for a in a_ab]
    for mask in levels[1:]:
        x = [_mm(jnp.where(mask, a, zero).astype(BF16), bd(t)) for a, t in zip(a_ab, t_inv)]
        yield
        t_inv = [t - _mm(t.astype(BF16), bd(xi)) for t, xi in zip(t_inv, x)]
        yield
    twu = [_mm(t.astype(BF16), jnp.concatenate([bd(o["kq"]), bd(a[:c])], axis=1))
           for t, o, a in zip(t_inv, ops, av)]
    yield
    arb = [_mm(ar, jnp.concatenate([bd(x[:, :LANES]), bd(x[:, LANES:])], axis=1))
           for ar, x in zip(a_rb, twu)]
    yield
    out = []
    for o, a, x, y in zip(ops, av, twu, arb):
        w = x[:, :LANES].astype(BF16)
        u0 = x[:, LANES:]
        pm = -jnp.where(same_head, _tn(w, o["bh"]), zero_sq)
        dd = jnp.where(same_head,
                       _tn(jnp.concatenate([o["v"], -u0], axis=0).astype(BF16),
                           jnp.concatenate([o["kh"], o["bh"]], axis=0)), zero_sq)
        out.append((o["rq"] - y[:, :LANES], a[c:] - y[:, LANES:], pm, dd, o["g"]))
    return out


def _spread(fillers, every=2):
    out = []
    for f in fillers:
        out.append(f)
        out.extend([lambda: None] * (every - 1))
    return out


def _interleave(gen, fillers):
    fillers = list(fillers)
    while True:
        try:
            next(gen)
        except StopIteration as stop:
            result = stop.value
            break
        if fillers:
            fillers.pop(0)()
    for f in fillers:
        f()
    return result


def _chunk_apply(pre, states):
    sb = [s.astype(BF16) for s in states]
    ys = [_nt(p[0].astype(BF16), s) + p[1] for p, s in zip(pre, sb)]
    new = [st * p[4] + _mm(s, p[2].astype(BF16)) + p[3] for p, s, st in zip(pre, sb, states)]
    return ys, new


def _rwkv_kernel(*refs, first_layer, d_rwkv):
    if first_layer:
        (zr_ref, zl_ref, wl_ref, vec_ref, out_ref, vfirst_out_ref,
         r_s, lw_s, k_s, v_s, kk_s, b_s, g_s, y_s, state_s) = refs
        vfirst_ref = None
    else:
        (zr_ref, zl_ref, vfirst_ref, wl_ref, vec_ref, out_ref,
         r_s, lw_s, k_s, v_s, kk_s, b_s, g_s, y_s, state_s) = refs
    tc = zr_ref.shape[0]
    n_pairs = d_rwkv // LANES
    c = CHUNK
    sub = RWKV_SUB
    sub_chunks = sub // c

    @pl.when(pl.program_id(1) == 0)
    def _():
        state_s[...] = jnp.zeros_like(state_s)

    w0, a0, mv0, k_k, k_a, r_k, gn_w, gn_b = (vec_ref[j:j + 1, :] for j in range(8))
    hrow = lax.broadcasted_iota(jnp.int32, (2 * LANES, 1), 0)
    hcol = lax.broadcasted_iota(jnp.int32, (1, 2 * LANES), 1)
    same = jnp.where((hrow // HEAD_DIM) == (hcol // HEAD_DIM), 1.0, 0.0).astype(BF16)

    def head_sum(x):
        xb = x.astype(BF16)
        w = 2 * LANES
        return jnp.concatenate([_mm(xb[:, j * w:(j + 1) * w], same) for j in range(d_rwkv // w)],
                               axis=1)

    lane = lax.broadcasted_iota(jnp.int32, (1, LORA_PAD), 1)
    o1 = LORA_DECAY
    o2 = o1 + LORA_AAA
    o3 = o2 + LORA_GATE

    def prepare(rows):
        zl = zl_ref[rows, :]
        act = jnp.where(lane < o1, jnp.tanh(zl),
                        jnp.where((lane >= o2) & (lane < o3), jax.nn.sigmoid(zl), zl))
        lora = _mm(act.astype(BF16), wl_ref[...])
        r = zr_ref[rows, 0:d_rwkv]
        k = zr_ref[rows, d_rwkv:2 * d_rwkv]
        v = zr_ref[rows, 2 * d_rwkv:3 * d_rwkv]
        lw = -math.exp(-0.5) * jax.nn.sigmoid(w0 + lora[:, 0:d_rwkv])
        a = jax.nn.sigmoid(a0 + lora[:, d_rwkv:2 * d_rwkv])
        if first_layer:
            vfirst_out_ref[rows, :] = v
        else:
            vgate = jax.nn.sigmoid(mv0 + lora[:, 3 * d_rwkv:4 * d_rwkv])
            v = v + (vfirst_ref[rows, :] - v) * vgate
        kk = k * k_k
        kk = kk / jnp.maximum(jnp.sqrt(head_sum(kk * kk)), 1e-12)
        r_s[rows, :] = r
        lw_s[rows, :] = lw
        k_s[rows, :] = k * (1.0 + (a - 1.0) * k_a)
        v_s[rows, :] = v
        kk_s[rows, :] = kk
        b_s[rows, :] = kk * a
        g_s[rows, :] = lora[:, 2 * d_rwkv:3 * d_rwkv]

    def problems(h):
        ins = []
        for ci in range(h * sub_chunks, (h + 1) * sub_chunks):
            rows = slice(ci * c, (ci + 1) * c)
            for p in range(n_pairs):
                ls = slice(p * LANES, (p + 1) * LANES)
                ins.append(tuple(ref[rows, ls] for ref in (r_s, lw_s, k_s, v_s, kk_s, b_s)))
        return ins

    states = [state_s[p] for p in range(n_pairs)]

    def chain_steps(h, pre):
        def step(j):
            def run():
                ys, new = _chunk_apply(pre[j * n_pairs:(j + 1) * n_pairs], states)
                states[:] = new
                ci = h * sub_chunks + j
                for p in range(n_pairs):
                    y_s[ci * c:(ci + 1) * c, p * LANES:(p + 1) * LANES] = ys[p]
            return run

        rows = slice(h * sub, (h + 1) * sub)
        tmp = {}

        def stats():
            y = y_s[rows, :]
            tmp["dlt"] = y - head_sum(y) * (1.0 / HEAD_DIM)
            tmp["bonus"] = head_sum(r_s[rows, :] * k_s[rows, :] * r_k) * v_s[rows, :]

        def finish():
            dlt = tmp["dlt"]
            var = head_sum(dlt * dlt) * (1.0 / HEAD_DIM)
            yn = dlt * lax.rsqrt(var + GN_EPS) * gn_w + gn_b
            out_ref[rows, :] = ((yn + tmp["bonus"]) * g_s[rows, :]).astype(out_ref.dtype)

        return [step(j) for j in range(sub_chunks)] + [stats, finish]

    consts = _chunk_consts()
    n_sub = tc // sub
    for h in range(n_sub):
        prepare(slice(h * sub, (h + 1) * sub))
    pending = []
    for h in range(n_sub):
        pre = _interleave(_chunk_precompute(problems(h), consts), _spread(pending))
        pending = chain_steps(h, pre)
    for f in pending:
        f()
    for p in range(n_pairs):
        state_s[p] = states[p]


def _rwkv(zr, zl, vfirst, wl, vecs, *, batch, seq, tc=RWKV_TC):
    t = zr.shape[0]
    d_rwkv = zr.shape[1] // 3
    first_layer = vfirst is None
    tiles = seq // tc
    row = lambda b, i: (b * tiles + i, 0)
    const = lambda b, i: (0, 0)
    in_specs = [pl.BlockSpec((tc, 3 * d_rwkv), row), pl.BlockSpec((tc, LORA_PAD), row)]
    args = [zr, zl]
    if not first_layer:
        in_specs.append(pl.BlockSpec((tc, d_rwkv), row))
        args.append(vfirst)
    in_specs += [pl.BlockSpec(wl.shape, const), pl.BlockSpec(vecs.shape, const)]
    args += [wl, vecs]
    out_specs = [pl.BlockSpec((tc, d_rwkv), row)]
    out_shape = [jax.ShapeDtypeStruct((t, d_rwkv), BF16)]
    if first_layer:
        out_specs.append(pl.BlockSpec((tc, d_rwkv), row))
        out_shape.append(jax.ShapeDtypeStruct((t, d_rwkv), F32))
    kern = functools.partial(_rwkv_kernel, first_layer=first_layer, d_rwkv=d_rwkv)
    res = pl.pallas_call(
        kern,
        grid=(batch, tiles),
        in_specs=in_specs,
        out_specs=out_specs,
        out_shape=out_shape,
        scratch_shapes=[pltpu.VMEM((tc, d_rwkv), F32) for _ in range(8)]
        + [pltpu.VMEM((d_rwkv // LANES, LANES, LANES), F32)],
        compiler_params=pltpu.CompilerParams(
            dimension_semantics=("arbitrary", "arbitrary"), vmem_limit_bytes=VMEM_LIMIT),
        name="rwkv7_first" if first_layer else "rwkv7",
    )(*args)
    return (res[0], res[1]) if first_layer else (res[0], vfirst)


def _mix_ffn_kernel(a_ref, rw_ref, x_ref, gains_ref, ga_ref, wo_ref, wu_ref, wd_ref, out_ref,
                    acc_ref, *, ff_chunk):
    n_pairs = a_ref.shape[0]
    tm = x_ref.shape[0]
    g_post, g_pre, g_ffn = (gains_ref[j:j + 1, :] for j in range(3))
    d_attn = n_pairs * LANES
    d_ff = wu_ref.shape[1]
    halves = [slice(j * MIX_SUB, (j + 1) * MIX_SUB) for j in range(tm // MIX_SUB)]
    x1s, hs = [], []
    for rows in halves:
        attn = jnp.concatenate([a_ref[p, rows, :].astype(F32) for p in range(n_pairs)], axis=-1)
        attn = _rms(attn, ga_ref[...]).astype(BF16)
        mixed = _mm(attn, wo_ref[0:d_attn, :]) + _mm(rw_ref[rows, :], wo_ref[d_attn:, :])
        x1 = x_ref[rows, :] + _rms(mixed, g_post)
        x1s.append(x1)
        hs.append(_rms(x1, g_pre).astype(BF16))
    for rows, h in zip(halves, hs):
        for c in range(d_ff // ff_chunk):
            cs = slice(c * ff_chunk, (c + 1) * ff_chunk)
            u = jnp.maximum(_mm(h, wu_ref[:, cs]), 0.0)
            part = _mm((u * u).astype(BF16), wd_ref[cs, :])
            if c == 0:
                acc_ref[rows, :] = part
            else:
                acc_ref[rows, :] += part
    for rows, x1 in zip(halves, x1s):
        out_ref[rows, :] = x1 + _rms(acc_ref[rows, :], g_ffn)


def _mix_ffn(attn, rw, x2, gains, ga, wo, wu, wd, *, layer, tm=MIX_TM, ff_chunk=1024):
    t, d = x2.shape
    n_pairs = attn.shape[0]
    d_rwkv = rw.shape[1]
    row = lambda i: (i, 0)
    const = lambda i: (0, 0)
    resident = lambda shape: pl.BlockSpec((None,) + shape[1:], lambda i: (layer, 0, 0),
                                          pipeline_mode=pl.Buffered(1))
    return pl.pallas_call(
        functools.partial(_mix_ffn_kernel, ff_chunk=ff_chunk),
        grid=(t // tm,),
        in_specs=[pl.BlockSpec((n_pairs, tm, LANES), lambda i: (0, i, 0)),
                  pl.BlockSpec((tm, d_rwkv), row), pl.BlockSpec((tm, d), row),
                  pl.BlockSpec(gains.shape, const), pl.BlockSpec(ga.shape, const),
                  resident(wo.shape), resident(wu.shape), resident(wd.shape)],
        out_specs=pl.BlockSpec((tm, d), row),
        out_shape=jax.ShapeDtypeStruct((t, d), F32),
        scratch_shapes=[pltpu.VMEM((tm, d), F32)],
        compiler_params=pltpu.CompilerParams(
            dimension_semantics=("arbitrary",), vmem_limit_bytes=VMEM_LIMIT),
        name="mix_ffn",
    )(attn, rw, x2, gains, ga, wo, wu, wd)


def _layer_params(i, d_attn, d_rwkv, w_in_first, w_in_rest, mu_shift, mu_shift_mv,
                  decay_up, aaa_up, gate_up, mv_up):
    w = w_in_first if i == 0 else w_in_rest[i - 1]
    d = w.shape[0]
    n_first = w_in_first.shape[1]
    lora_used = LORA_DECAY + LORA_AAA + LORA_GATE + LORA_MV
    n_pairs = d_attn // LANES
    qkv = [w[:, j * d_attn + p * LANES:j * d_attn + (p + 1) * LANES]
           for p in range(n_pairs) for j in range(3)]
    pieces = qkv + [w[:, 3 * d_attn:n_first]]
    mus = [mu_shift[i]]
    if i == 0:
        pieces.append(jnp.zeros((d, LORA_MV), F32))
        mus.append(jnp.zeros((LORA_MV,), F32))
    else:
        pieces.append(w[:, n_first:])
        mus.append(mu_shift_mv[i - 1])
    pieces.append(jnp.zeros((d, LORA_PAD - lora_used), F32))
    mus.append(jnp.zeros((LORA_PAD - lora_used,), F32))
    w_r = jnp.concatenate(pieces, axis=1).astype(BF16)
    mu_r = jnp.concatenate(mus)[None, :]
    wl = jnp.zeros((LORA_PAD, 4 * d_rwkv), F32)
    o1 = LORA_DECAY
    o2 = o1 + LORA_AAA
    o3 = o2 + LORA_GATE
    wl = wl.at[0:o1, 0:d_rwkv].set(decay_up[i])
    wl = wl.at[o1:o2, d_rwkv:2 * d_rwkv].set(aaa_up[i])
    wl = wl.at[o2:o3, 2 * d_rwkv:3 * d_rwkv].set(gate_up[i])
    if i > 0:
        wl = wl.at[o3:o3 + LORA_MV, 3 * d_rwkv:4 * d_rwkv].set(mv_up[i - 1])
    return w_r, mu_r, wl.astype(BF16)


def kernel(x, norm_mix_pre, norm_mix_post, norm_ffn_pre, norm_ffn_post, w_in_first, w_in_rest,
           mu_shift, mu_shift_mv, attn_out_gain, decay_w0, decay_up, aaa_a0, aaa_up, mv_v0, mv_up,
           gate_up, k_k, k_a, r_k, gn_w, gn_b, w_out, w_ffn_up, w_ffn_down):
    batch, seq, d = x.shape
    depth = norm_mix_pre.shape[0]
    d_attn = attn_out_gain.shape[1]
    d_rwkv = decay_w0.shape[1]
    x2 = x.reshape(batch * seq, d)
    wo_all, wu_all, wd_all = (w.astype(BF16) for w in (w_out, w_ffn_up, w_ffn_down))
    vfirst = None
    for i in range(depth):
        w_r, mu_r, wl = _layer_params(i, d_attn, d_rwkv, w_in_first, w_in_rest, mu_shift,
                                      mu_shift_mv, decay_up, aaa_up, gate_up, mv_up)
        mv0 = mv_v0[i - 1] if i > 0 else jnp.zeros((d_rwkv,), F32)
        vecs = jnp.stack([decay_w0[i], aaa_a0[i], mv0, k_k[i], k_a[i], r_k[i].reshape(-1),
                          gn_w[i], gn_b[i]])
        x1, x4, x16, zr, zl = _inproj(x2, norm_mix_pre[i][None, :], w_r, mu_r, batch=batch, seq=seq)
        attn = _attention(x1, x4, x16, batch=batch, seq=seq)
        rw, vfirst = _rwkv(zr, zl, vfirst, wl, vecs, batch=batch, seq=seq)
        gains = jnp.stack([norm_mix_post[i], norm_ffn_pre[i], norm_ffn_post[i]])
        x2 = _mix_ffn(attn, rw, x2, gains, attn_out_gain[i][None, :], wo_all, wu_all, wd_all,
                      layer=i)
    return x2.reshape(batch, seq, d)
```

```python
import functools
import math

import jax
import jax.numpy as jnp
from jax import lax
from jax.experimental import pallas as pl
from jax.experimental.pallas import tpu as pltpu

F32 = jnp.float32
BF16 = jnp.bfloat16

HEAD_DIM = 64
LANES = 128
NORM_EPS = 1e-6
GN_EPS = 64e-5
DILATIONS = (1, 4, 16)
BAND = 128
QKV_W = 3 * LANES
INPROJ_TM = BAND * DILATIONS[1]
ATTN_GROUP = 1
LORA_DECAY, LORA_AAA, LORA_GATE, LORA_MV = 32, 32, 96, 32
LORA_PAD = 256
MIX_TM = 1024
MIX_SUB = 256
CHUNK = 64
RWKV_TC = 1024
RWKV_SUB = 256
NEG_BIG = -1e30
VMEM_LIMIT = 56 * 1024 * 1024


def _nt(a, b):
    return lax.dot_general(a, b, (((1,), (1,)), ((), ())), preferred_element_type=F32)


def _tn(a, b):
    return lax.dot_general(a, b, (((0,), (0,)), ((), ())), preferred_element_type=F32)


def _mm(a, b):
    return jnp.dot(a, b, preferred_element_type=F32)


def _rms(x, g):
    return x * lax.rsqrt(jnp.mean(x * x, axis=-1, keepdims=True) + NORM_EPS) * g


def _inproj_kernel(x_ref, g_ref, w_ref, mu_ref, x1_ref, x4_ref, x16_ref, zr_ref, zl_ref,
                   carry_ref, zs_ref, z4_ref, *, seq_tiles, n_pairs, d_rwkv3):
    i = pl.program_id(0)
    tm = x_ref.shape[0]
    h = _rms(x_ref[...], g_ref[...]).astype(BF16)
    d_attn3 = n_pairs * QKV_W
    f = DILATIONS[1]

    def attn_pairs(p0):
        zz = _mm(h, w_ref[:, p0 * QKV_W:(p0 + 2) * QKV_W])
        for p in (p0, p0 + 1):
            z = zz[:, (p - p0) * QKV_W:(p - p0 + 1) * QKV_W]
            x1_ref[p, :, 0] = z.astype(BF16).reshape(tm // BAND, BAND, QKV_W)
            for j in range(QKV_W // LANES):
                ls = slice(j * LANES, (j + 1) * LANES)
                zs_ref[p, j] = z[:, ls]
                for b in range(f):
                    z4 = zs_ref[p, j, pl.ds(b, tm // f, stride=f), :]
                    x4_ref[p, b, :, ls] = z4.astype(BF16)
                    z4_ref[p, j, b] = z4
                    for a in range(f):
                        z16 = z4_ref[p, j, b, pl.ds(a, tm // (f * f), stride=f), :]
                        x16_ref[p, f * a + b, :, ls] = z16.astype(BF16)

    first_tile = (i % seq_tiles) == 0
    row0 = lax.broadcasted_iota(jnp.int32, (tm, 1), 0) == 0

    def shifted(c0, w):
        z = _mm(h, w_ref[:, d_attn3 + c0:d_attn3 + c0 + w])
        carry = jnp.where(first_tile, 0.0, carry_ref[:, c0:c0 + w])
        prev = jnp.where(row0, carry, pltpu.roll(z, 1, 0))
        carry_ref[:, c0:c0 + w] = z[tm - 1:tm, :]
        zs = z + (prev - z) * mu_ref[:, c0:c0 + w]
        if c0 < d_rwkv3:
            zr_ref[:, c0:c0 + w] = zs
        else:
            zl_ref[...] = zs

    n_shift = d_rwkv3 + LORA_PAD
    cw = 512
    shift_tasks = [(c0, min(cw, n_shift - c0)) for c0 in range(0, n_shift, cw)]
    attn_tasks = list(range(0, n_pairs, 2))
    while attn_tasks or shift_tasks:
        if attn_tasks:
            attn_pairs(attn_tasks.pop(0))
        for _ in range(2):
            if shift_tasks:
                shifted(*shift_tasks.pop(0))


def _inproj(x2, g, w, mu, *, batch, seq):
    tm = INPROJ_TM
    t, d = x2.shape
    nc = w.shape[1]
    n_pairs = (d // 2) // LANES
    d_attn3 = n_pairs * QKV_W
    d_rwkv3 = nc - d_attn3 - LORA_PAD
    seq_tiles = seq // tm
    d4, d16 = DILATIONS[1], DILATIONS[2]
    assert tm == BAND * d4 and d16 == d4 * d4 and seq % (BAND * d16) == 0
    sub16 = BAND * d16 // tm
    kern = functools.partial(_inproj_kernel, seq_tiles=seq_tiles, n_pairs=n_pairs, d_rwkv3=d_rwkv3)
    return pl.pallas_call(
        kern,
        grid=(t // tm,),
        in_specs=[
            pl.BlockSpec((tm, d), lambda i: (i, 0)),
            pl.BlockSpec((1, d), lambda i: (0, 0)),
            pl.BlockSpec((d, nc), lambda i: (0, 0)),
            pl.BlockSpec((1, nc - d_attn3), lambda i: (0, 0)),
        ],
        out_specs=[
            pl.BlockSpec((n_pairs, None, tm // BAND, 1, BAND, QKV_W),
                         lambda i: (0, i // seq_tiles, i % seq_tiles, 0, 0, 0)),
            pl.BlockSpec((n_pairs, None, None, d4, BAND, QKV_W),
                         lambda i: (0, i // seq_tiles, i % seq_tiles, 0, 0, 0)),
            pl.BlockSpec((n_pairs, None, None, d16, tm // d16, QKV_W),
                         lambda i: (0, i // seq_tiles, (i % seq_tiles) // sub16, 0,
                                    (i % seq_tiles) % sub16, 0)),
            pl.BlockSpec((tm, d_rwkv3), lambda i: (i, 0)),
            pl.BlockSpec((tm, LORA_PAD), lambda i: (i, 0)),
        ],
        out_shape=[
            jax.ShapeDtypeStruct((n_pairs, batch, seq // BAND, 1, BAND, QKV_W), BF16),
            jax.ShapeDtypeStruct((n_pairs, batch, seq // (BAND * d4), d4, BAND, QKV_W), BF16),
            jax.ShapeDtypeStruct((n_pairs, batch, seq // (BAND * d16), d16, BAND, QKV_W), BF16),
            jax.ShapeDtypeStruct((t, d_rwkv3), F32),
            jax.ShapeDtypeStruct((t, LORA_PAD), F32),
        ],
        scratch_shapes=[pltpu.VMEM((1, nc - d_attn3), F32),
                        pltpu.VMEM((n_pairs, QKV_W // LANES, tm, LANES), F32),
                        pltpu.VMEM((n_pairs, QKV_W // LANES, d4, tm // d4, LANES), F32)],
        compiler_params=pltpu.CompilerParams(
            dimension_semantics=("arbitrary",), vmem_limit_bytes=VMEM_LIMIT),
        name="inproj",
    )(x2, g, w, mu)


def _attn_scores(x_ref, probs, head0):
    scale = 1.0 / math.sqrt(HEAD_DIM)
    out = []
    for r, n in probs:
        prev = max(n - 1, 0)
        q = x_ref[n, r, :, 0:LANES] * scale
        kw = jnp.concatenate([x_ref[prev, r, :, LANES:2 * LANES], x_ref[n, r, :, LANES:2 * LANES]],
                             axis=0)
        out.append(_nt(_block_diag(q, head0), kw))
    return out


def _attn_softmax(probs, scores, consts):
    head0, bias_first, bias_rest = consts
    s = []
    for (r, n), si in zip(probs, scores):
        bias = bias_first if n == 0 else bias_rest
        s.append(jnp.concatenate([si[:BAND] + bias, si[BAND:] + bias], axis=0))
    m = [jnp.max(si, axis=-1, keepdims=True) for si in s]
    p = [jnp.exp(si - mi).astype(BF16) for si, mi in zip(s, m)]
    return p, [jnp.where(head0, mi[:BAND], mi[BAND:]) for mi in m]


def _attn_pv(x_ref, probs, p, head0):
    ones = jnp.ones((2 * BAND, LANES), BF16)
    pv = []
    for (r, n), pi in zip(probs, p):
        prev = max(n - 1, 0)
        vw = jnp.concatenate([x_ref[prev, r, :, 2 * LANES:], x_ref[n, r, :, 2 * LANES:]], axis=0)
        pv.append(_mm(pi, jnp.concatenate([vw, ones], axis=1)))
    return [tuple(jnp.where(head0, t[:BAND], t[BAND:]) for t in (pvi[:, :LANES], pvi[:, LANES:]))
            for pvi in pv]


def _attn_kernel(x1_ref, x4_ref, x16_ref, out_ref, nat_s, ph_s, mid_s, p_scr):
    seq = out_ref.shape[0]
    lane = lax.broadcasted_iota(jnp.int32, (1, LANES), 1)
    head0 = lane < HEAD_DIM
    qrow = lax.broadcasted_iota(jnp.int32, (BAND, 1), 0)
    kcol = lax.broadcasted_iota(jnp.int32, (1, 2 * BAND), 1)
    rel = BAND + qrow - kcol
    bias_rest = jnp.where((rel >= 0) & (rel <= BAND), 0.0, NEG_BIG).astype(F32)
    bias_first = jnp.where(kcol < BAND, NEG_BIG, bias_rest)
    consts = (head0, bias_first, bias_rest)
    group = ATTN_GROUP
    n_groups = seq // (BAND * group)

    for bi, (d, x_ref) in reversed(list(enumerate(zip(DILATIONS, (x1_ref, x4_ref, x16_ref))))):
        n_blocks = seq // (BAND * d)
        shift = n_blocks.bit_length() - 1
        sub_len = seq // d
        dst = nat_s.at[bi] if d == 1 else ph_s

        def probs_of(g, shift=shift, n_blocks=n_blocks):
            js = [g * group + i for i in range(group)]
            return [(j >> shift, j & (n_blocks - 1)) for j in js], js

        def store(js, q, vals, dst=dst):
            for j, t in zip(js, vals):
                dst[q, j * BAND:(j + 1) * BAND, :] = t

        cur, js = probs_of(0)
        p, m = _attn_softmax(cur, _attn_scores(x_ref, cur, head0), consts)
        store(js, 1, m)
        for i, pi in enumerate(p):
            p_scr[i] = pi
        for g in range(n_groups):
            last = g + 1 == n_groups
            if not last:
                nxt, js_next = probs_of(g + 1)
                s_next = _attn_scores(x_ref, nxt, head0)
            acc_l = _attn_pv(x_ref, cur, [p_scr[i] for i in range(group)], head0)
            store(js, 0, [t[0] for t in acc_l])
            store(js, 2, [t[1] for t in acc_l])
            if not last:
                p, m = _attn_softmax(nxt, s_next, consts)
                store(js_next, 1, m)
                for i, pi in enumerate(p):
                    p_scr[i] = pi
                cur, js = nxt, js_next
        f = DILATIONS[1]
        if d == f:
            for r in range(d):
                staged = slice(r * sub_len, (r + 1) * sub_len)
                for q in range(3):
                    nat_s[bi, q, pl.ds(r, sub_len, stride=f), :] = ph_s[q, staged, :]
        elif d == f * f:
            for q in range(3):
                for b in range(f):
                    for a in range(f):
                        staged = slice((f * a + b) * sub_len, (f * a + b + 1) * sub_len)
                        mid_s[pl.ds(b * f * sub_len + a, sub_len, stride=f), :] = ph_s[q, staged, :]
                for b in range(f):
                    rows = slice(b * f * sub_len, (b + 1) * f * sub_len)
                    nat_s[bi, q, pl.ds(b, f * sub_len, stride=f), :] = mid_s[rows, :]

    rows_c = 512
    for c in range(seq // rows_c):
        rows = slice(c * rows_c, (c + 1) * rows_c)
        ms = [nat_s[bi, 1, rows, :] for bi in range(3)]
        top = jnp.maximum(jnp.maximum(ms[0], ms[1]), ms[2])
        es = [jnp.exp(mb - top) for mb in ms]
        num = sum(e * nat_s[bi, 0, rows, :] for bi, e in enumerate(es))
        den = sum(e * nat_s[bi, 2, rows, :] for bi, e in enumerate(es))
        out_ref[rows, :] = (num / den).astype(out_ref.dtype)


def _attention(x1, x4, x16, *, batch, seq):
    n_pairs = x1.shape[0]
    spec = lambda x: pl.BlockSpec((None, None) + x.shape[2:], lambda b, p: (p, b, 0, 0, 0, 0))
    return pl.pallas_call(
        _attn_kernel,
        grid=(batch, n_pairs),
        in_specs=[spec(x1), spec(x4), spec(x16)],
        out_specs=pl.BlockSpec((None, seq, LANES), lambda b, p: (p, b, 0)),
        out_shape=jax.ShapeDtypeStruct((n_pairs, batch * seq, LANES), BF16),
        scratch_shapes=[pltpu.VMEM((3, 3, seq, LANES), F32), pltpu.VMEM((3, seq, LANES), F32),
                        pltpu.VMEM((seq, LANES), F32),
                        pltpu.VMEM((ATTN_GROUP, 2 * BAND, 2 * BAND), BF16)],
        compiler_params=pltpu.CompilerParams(
            dimension_semantics=("arbitrary", "arbitrary"), vmem_limit_bytes=VMEM_LIMIT),
        name="attention",
    )(x1, x4, x16)


def _block_diag(x, head0):
    zero = jnp.zeros_like(x)
    return jnp.concatenate([jnp.where(head0, x, zero), jnp.where(head0, zero, x)], axis=0)


def _chunk_consts():
    c = CHUNK
    trow = lax.broadcasted_iota(jnp.int32, (c, 1), 0)
    col3 = lax.broadcasted_iota(jnp.int32, (1, 3 * c), 1)
    tri3 = jnp.where((col3 & (c - 1)) <= trow, 1.0, 0.0).astype(BF16)
    lane_w = lax.broadcasted_iota(jnp.int32, (1, LANES), 1)
    scol = lane_w & (HEAD_DIM - 1)
    strict = scol < trow
    incl = scol <= trow
    eye_w = jnp.where(scol == trow, 1.0, 0.0).astype(F32)
    head0 = lane_w < HEAD_DIM
    rr = lax.broadcasted_iota(jnp.int32, (LANES, 1), 0)
    same_head = (rr // HEAD_DIM) == (lane_w // HEAD_DIM)
    levels = []
    s = 1
    while s < c:
        in_pair = (trow // (2 * s)) == (scol // (2 * s))
        levels.append(in_pair & ((trow // s) != (scol // s)) & strict)
        s *= 2
    return dict(tri3=tri3, strict=strict, incl=incl, eye_w=eye_w, head0=head0,
                same_head=same_head, levels=levels)


def _chunk_precompute(ins, consts):
    c = CHUNK
    head0 = consts["head0"]
    strict, incl, same_head = consts["strict"], consts["incl"], consts["same_head"]
    levels = consts["levels"]
    zero = jnp.zeros((c, LANES), F32)
    zero_sq = jnp.zeros((LANES, LANES), F32)

    def bd(x):
        return _block_diag(x, head0).astype(BF16)

    splits = []
    for (_, lw, _, _, _, _) in ins:
        hi = lw.astype(BF16)
        r1 = lw - hi.astype(F32)
        mid = r1.astype(BF16)
        lo = (r1 - mid.astype(F32)).astype(BF16)
        splits.append(jnp.concatenate([hi, mid, lo], axis=0))
    cums = []
    for i in range(0, len(splits), 2):
        both = _mm(consts["tri3"], jnp.concatenate(splits[i:i + 2], axis=1))
        cums += [both[:, :LANES], both[:, LANES:]]
    yield
    ops = []
    for (r, lw, k, v, kk, b), cum in zip(ins, cums):
        cum_end = cum[c - 1:c, :]
        e_neg = jnp.exp(-cum)
        e_end = jnp.exp(cum_end - cum)
        ops.append(dict(kq=kk * jnp.exp(cum - lw), rq=r * jnp.exp(cum), kt=k * e_neg, bt=b * e_neg,
                        kh=(k * e_end).astype(BF16), bh=(b * e_end).astype(BF16),
                        g=jnp.exp(cum_end), v=v))
    a_all = [_nt(jnp.concatenate([o["kq"], o["rq"]], axis=0).astype(BF16),
                 jnp.concatenate([bd(o["bt"]), bd(o["kt"])], axis=0)) for o in ops]
    a_ab = [jnp.where(strict, a[:c, :LANES], zero) for a in a_all]
    a_rb = [jnp.where(incl, a[c:, :LANES], zero).astype(BF16) for a in a_all]
    av = [_mm(jnp.concatenate([jnp.where(strict, a[:c, LANES:], zero),
                               jnp.where(incl, a[c:, LANES:], zero)], axis=0).astype(BF16),
              bd(o["v"])) for a, o in zip(a_all, ops)]
    yield
    t_inv = [consts["eye_w"] - jnp.where(levels[0], a, zero) for a in a_ab]
    for mask in levels[1:]:
        x = [_mm(jnp.where(mask, a, zero).astype(BF16), bd(t)) for a, t in zip(a_ab, t_inv)]
        yield
        t_inv = [t - _mm(t.astype(BF16), bd(xi)) for t, xi in zip(t_inv, x)]
        yield
    twu = [_mm(t.astype(BF16), jnp.concatenate([bd(o["kq"]), bd(a[:c])], axis=1))
           for t, o, a in zip(t_inv, ops, av)]
    yield
    arb = [_mm(ar, jnp.concatenate([bd(x[:, :LANES]), bd(x[:, LANES:])], axis=1))
           for ar, x in zip(a_rb, twu)]
    yield
    out = []
    for o, a, x, y in zip(ops, av, twu, arb):
        w = x[:, :LANES].astype(BF16)
        u0 = x[:, LANES:]
        pm = -jnp.where(same_head, _tn(w, o["bh"]), zero_sq)
        dd = jnp.where(same_head,
                       _tn(jnp.concatenate([o["v"], -u0], axis=0).astype(BF16),
                           jnp.concatenate([o["kh"], o["bh"]], axis=0)), zero_sq)
        out.append((o["rq"] - y[:, :LANES], a[c:] - y[:, LANES:], pm, dd, o["g"]))
    return out


def _spread(fillers, every=2):
    out = []
    for f in fillers:
        out.append(f)
        out.extend([lambda: None] * (every - 1))
    return out


def _interleave(gen, fillers):
    fillers = list(fillers)
    while True:
        try:
            next(gen)
        except StopIteration as stop:
            result = stop.value
            break
        if fillers:
            fillers.pop(0)()
    for f in fillers:
        f()
    return result


def _chunk_apply(pre, states):
    sb = [s.astype(BF16) for s in states]
    ys = [_nt(p[0].astype(BF16), s) + p[1] for p, s in zip(pre, sb)]
    new = [st * p[4] + _mm(s, p[2].astype(BF16)) + p[3] for p, s, st in zip(pre, sb, states)]
    return ys, new


def _rwkv_kernel(*refs, first_layer, d_rwkv):
    if first_layer:
        (zr_ref, zl_ref, wl_ref, vec_ref, out_ref, vfirst_out_ref,
         r_s, lw_s, k_s, v_s, kk_s, b_s, g_s, y_s, state_s) = refs
        vfirst_ref = None
    else:
        (zr_ref, zl_ref, vfirst_ref, wl_ref, vec_ref, out_ref,
         r_s, lw_s, k_s, v_s, kk_s, b_s, g_s, y_s, state_s) = refs
    tc = zr_ref.shape[0]
    n_pairs = d_rwkv // LANES
    c = CHUNK
    sub = RWKV_SUB
    sub_chunks = sub // c

    @pl.when(pl.program_id(1) == 0)
    def _():
        state_s[...] = jnp.zeros_like(state_s)

    w0, a0, mv0, k_k, k_a, r_k, gn_w, gn_b = (vec_ref[j:j + 1, :] for j in range(8))
    hrow = lax.broadcasted_iota(jnp.int32, (2 * LANES, 1), 0)
    hcol = lax.broadcasted_iota(jnp.int32, (1, 2 * LANES), 1)
    same = jnp.where((hrow // HEAD_DIM) == (hcol // HEAD_DIM), 1.0, 0.0).astype(BF16)

    def head_sum(x):
        xb = x.astype(BF16)
        w = 2 * LANES
        return jnp.concatenate([_mm(xb[:, j * w:(j + 1) * w], same) for j in range(d_rwkv // w)],
                               axis=1)

    lane = lax.broadcasted_iota(jnp.int32, (1, LORA_PAD), 1)
    o1 = LORA_DECAY
    o2 = o1 + LORA_AAA
    o3 = o2 + LORA_GATE

    def prepare(rows):
        zl = zl_ref[rows, :]
        act = jnp.where(lane < o1, jnp.tanh(zl),
                        jnp.where((lane >= o2) & (lane < o3), jax.nn.sigmoid(zl), zl))
        lora = _mm(act.astype(BF16), wl_ref[...])
        r = zr_ref[rows, 0:d_rwkv]
        k = zr_ref[rows, d_rwkv:2 * d_rwkv]
        v = zr_ref[rows, 2 * d_rwkv:3 * d_rwkv]
        lw = -math.exp(-0.5) * jax.nn.sigmoid(w0 + lora[:, 0:d_rwkv])
        a = jax.nn.sigmoid(a0 + lora[:, d_rwkv:2 * d_rwkv])
        if first_layer:
            vfirst_out_ref[rows, :] = v
        else:
            vgate = jax.nn.sigmoid(mv0 + lora[:, 3 * d_rwkv:4 * d_rwkv])
            v = v + (vfirst_ref[rows, :] - v) * vgate
        kk = k * k_k
        kk = kk / jnp.maximum(jnp.sqrt(head_sum(kk * kk)), 1e-12)
        r_s[rows, :] = r
        lw_s[rows, :] = lw
        k_s[rows, :] = k * (1.0 + (a - 1.0) * k_a)
        v_s[rows, :] = v
        kk_s[rows, :] = kk
        b_s[rows, :] = kk * a
        g_s[rows, :] = lora[:, 2 * d_rwkv:3 * d_rwkv]

    def problems(h):
        ins = []
        for ci in range(h * sub_chunks, (h + 1) * sub_chunks):
            rows = slice(ci * c, (ci + 1) * c)
            for p in range(n_pairs):
                ls = slice(p * LANES, (p + 1) * LANES)
                ins.append(tuple(ref[rows, ls] for ref in (r_s, lw_s, k_s, v_s, kk_s, b_s)))
        return ins

    states = [state_s[p] for p in range(n_pairs)]

    def chain_steps(h, pre):
        def step(j):
            def run():
                ys, new = _chunk_apply(pre[j * n_pairs:(j + 1) * n_pairs], states)
                states[:] = new
                ci = h * sub_chunks + j
                for p in range(n_pairs):
                    y_s[ci * c:(ci + 1) * c, p * LANES:(p + 1) * LANES] = ys[p]
            return run

        rows = slice(h * sub, (h + 1) * sub)
        tmp = {}

        def stats():
            y = y_s[rows, :]
            tmp["dlt"] = y - head_sum(y) * (1.0 / HEAD_DIM)
            tmp["bonus"] = head_sum(r_s[rows, :] * k_s[rows, :] * r_k) * v_s[rows, :]

        def finish():
            dlt = tmp["dlt"]
            var = head_sum(dlt * dlt) * (1.0 / HEAD_DIM)
            yn = dlt * lax.rsqrt(var + GN_EPS) * gn_w + gn_b
            out_ref[rows, :] = ((yn + tmp["bonus"]) * g_s[rows, :]).astype(out_ref.dtype)

        return [step(j) for j in range(sub_chunks)] + [stats, finish]

    consts = _chunk_consts()
    n_sub = tc // sub
    for h in range(n_sub):
        prepare(slice(h * sub, (h + 1) * sub))
    pending = []
    for h in range(n_sub):
        pre = _interleave(_chunk_precompute(problems(h), consts), _spread(pending))
        pending = chain_steps(h, pre)
    for f in pending:
        f()
    for p in range(n_pairs):
        state_s[p] = states[p]


def _rwkv(zr, zl, vfirst, wl, vecs, *, batch, seq, tc=RWKV_TC):
    t = zr.shape[0]
    d_rwkv = zr.shape[1] // 3
    first_layer = vfirst is None
    tiles = seq // tc
    row = lambda b, i: (b * tiles + i, 0)
    const = lambda b, i: (0, 0)
    in_specs = [pl.BlockSpec((tc, 3 * d_rwkv), row), pl.BlockSpec((tc, LORA_PAD), row)]
    args = [zr, zl]
    if not first_layer:
        in_specs.append(pl.BlockSpec((tc, d_rwkv), row))
        args.append(vfirst)
    in_specs += [pl.BlockSpec(wl.shape, const), pl.BlockSpec(vecs.shape, const)]
    args += [wl, vecs]
    out_specs = [pl.BlockSpec((tc, d_rwkv), row)]
    out_shape = [jax.ShapeDtypeStruct((t, d_rwkv), BF16)]
    if first_layer:
        out_specs.append(pl.BlockSpec((tc, d_rwkv), row))
        out_shape.append(jax.ShapeDtypeStruct((t, d_rwkv), F32))
    kern = functools.partial(_rwkv_kernel, first_layer=first_layer, d_rwkv=d_rwkv)
    res = pl.pallas_call(
        kern,
        grid=(batch, tiles),
        in_specs=in_specs,
        out_specs=out_specs,
        out_shape=out_shape,
        scratch_shapes=[pltpu.VMEM((tc, d_rwkv), F32) for _ in range(8)]
        + [pltpu.VMEM((d_rwkv // LANES, LANES, LANES), F32)],
        compiler_params=pltpu.CompilerParams(
            dimension_semantics=("arbitrary", "arbitrary"), vmem_limit_bytes=VMEM_LIMIT),
        name="rwkv7_first" if first_layer else "rwkv7",
    )(*args)
    return (res[0], res[1]) if first_layer else (res[0], vfirst)


def _mix_ffn_kernel(a_ref, rw_ref, x_ref, gains_ref, ga_ref, wo_ref, wu_ref, wd_ref, out_ref,
                    acc_ref, *, ff_chunk):
    n_pairs = a_ref.shape[0]
    tm = x_ref.shape[0]
    g_post, g_pre, g_ffn = (gains_ref[j:j + 1, :] for j in range(3))
    d_attn = n_pairs * LANES
    d_ff = wu_ref.shape[1]
    halves = [slice(j * MIX_SUB, (j + 1) * MIX_SUB) for j in range(tm // MIX_SUB)]
    x1s, hs = [], []
    for rows in halves:
        attn = jnp.concatenate([a_ref[p, rows, :].astype(F32) for p in range(n_pairs)], axis=-1)
        attn = _rms(attn, ga_ref[...]).astype(BF16)
        mixed = _mm(attn, wo_ref[0:d_attn, :]) + _mm(rw_ref[rows, :], wo_ref[d_attn:, :])
        x1 = x_ref[rows, :] + _rms(mixed, g_post)
        x1s.append(x1)
        hs.append(_rms(x1, g_pre).astype(BF16))
    for rows, h in zip(halves, hs):
        for c in range(d_ff // ff_chunk):
            cs = slice(c * ff_chunk, (c + 1) * ff_chunk)
            u = jnp.maximum(_mm(h, wu_ref[:, cs]), 0.0)
            part = _mm((u * u).astype(BF16), wd_ref[cs, :])
            if c == 0:
                acc_ref[rows, :] = part
            else:
                acc_ref[rows, :] += part
    for rows, x1 in zip(halves, x1s):
        out_ref[rows, :] = x1 + _rms(acc_ref[rows, :], g_ffn)


def _mix_ffn(attn, rw, x2, gains, ga, wo, wu, wd, *, layer, tm=MIX_TM, ff_chunk=1024):
    t, d = x2.shape
    n_pairs = attn.shape[0]
    d_rwkv = rw.shape[1]
    row = lambda i: (i, 0)
    const = lambda i: (0, 0)
    resident = lambda shape: pl.BlockSpec((None,) + shape[1:], lambda i: (layer, 0, 0),
                                          pipeline_mode=pl.Buffered(1))
    return pl.pallas_call(
        functools.partial(_mix_ffn_kernel, ff_chunk=ff_chunk),
        grid=(t // tm,),
        in_specs=[pl.BlockSpec((n_pairs, tm, LANES), lambda i: (0, i, 0)),
                  pl.BlockSpec((tm, d_rwkv), row), pl.BlockSpec((tm, d), row),
                  pl.BlockSpec(gains.shape, const), pl.BlockSpec(ga.shape, const),
                  resident(wo.shape), resident(wu.shape), resident(wd.shape)],
        out_specs=pl.BlockSpec((tm, d), row),
        out_shape=jax.ShapeDtypeStruct((t, d), F32),
        scratch_shapes=[pltpu.VMEM((tm, d), F32)],
        compiler_params=pltpu.CompilerParams(
            dimension_semantics=("arbitrary",), vmem_limit_bytes=VMEM_LIMIT),
        name="mix_ffn",
    )(attn, rw, x2, gains, ga, wo, wu, wd)


def _layer_params(i, d_attn, d_rwkv, w_in_first, w_in_rest, mu_shift, mu_shift_mv,
                  decay_up, aaa_up, gate_up, mv_up):
    w = w_in_first if i == 0 else w_in_rest[i - 1]
    d = w.shape[0]
    n_first = w_in_first.shape[1]
    lora_used = LORA_DECAY + LORA_AAA + LORA_GATE + LORA_MV
    n_pairs = d_attn // LANES
    qkv = [w[:, j * d_attn + p * LANES:j * d_attn + (p + 1) * LANES]
           for p in range(n_pairs) for j in range(3)]
    pieces = qkv + [w[:, 3 * d_attn:n_first]]
    mus = [mu_shift[i]]
    if i == 0:
        pieces.append(jnp.zeros((d, LORA_MV), F32))
        mus.append(jnp.zeros((LORA_MV,), F32))
    else:
        pieces.append(w[:, n_first:])
        mus.append(mu_shift_mv[i - 1])
    pieces.append(jnp.zeros((d, LORA_PAD - lora_used), F32))
    mus.append(jnp.zeros((LORA_PAD - lora_used,), F32))
    w_r = jnp.concatenate(pieces, axis=1).astype(BF16)
    mu_r = jnp.concatenate(mus)[None, :]
    wl = jnp.zeros((LORA_PAD, 4 * d_rwkv), F32)
    o1 = LORA_DECAY
    o2 = o1 + LORA_AAA
    o3 = o2 + LORA_GATE
    wl = wl.at[0:o1, 0:d_rwkv].set(decay_up[i])
    wl = wl.at[o1:o2, d_rwkv:2 * d_rwkv].set(aaa_up[i])
    wl = wl.at[o2:o3, 2 * d_rwkv:3 * d_rwkv].set(gate_up[i])
    if i > 0:
        wl = wl.at[o3:o3 + LORA_MV, 3 * d_rwkv:4 * d_rwkv].set(mv_up[i - 1])
    return w_r, mu_r, wl.astype(BF16)


def kernel(x, norm_mix_pre, norm_mix_post, norm_ffn_pre, norm_ffn_post, w_in_first, w_in_rest,
           mu_shift, mu_shift_mv, attn_out_gain, decay_w0, decay_up, aaa_a0, aaa_up, mv_v0, mv_up,
           gate_up, k_k, k_a, r_k, gn_w, gn_b, w_out, w_ffn_up, w_ffn_down):
    batch, seq, d = x.shape
    depth = norm_mix_pre.shape[0]
    d_attn = attn_out_gain.shape[1]
    d_rwkv = decay_w0.shape[1]
    x2 = x.reshape(batch * seq, d)
    wo_all, wu_all, wd_all = (w.astype(BF16) for w in (w_out, w_ffn_up, w_ffn_down))
    vfirst = None
    for i in range(depth):
        w_r, mu_r, wl = _layer_params(i, d_attn, d_rwkv, w_in_first, w_in_rest, mu_shift,
                                      mu_shift_mv, decay_up, aaa_up, gate_up, mv_up)
        mv0 = mv_v0[i - 1] if i > 0 else jnp.zeros((d_rwkv,), F32)
        vecs = jnp.stack([decay_w0[i], aaa_a0[i], mv0, k_k[i], k_a[i], r_k[i].reshape(-1),
                          gn_w[i], gn_b[i]])
        x1, x4, x16, zr, zl = _inproj(x2, norm_mix_pre[i][None, :], w_r, mu_r, batch=batch, seq=seq)
        attn = _attention(x1, x4, x16, batch=batch, seq=seq)
        rw, vfirst = _rwkv(zr, zl, vfirst, wl, vecs, batch=batch, seq=seq)
        gains = jnp.stack([norm_mix_post[i], norm_ffn_pre[i], norm_ffn_post[i]])
        x2 = _mix_ffn(attn, rw, x2, gains, attn_out_gain[i][None, :], wo_all, wu_all, wd_all,
                      layer=i)
    return x2.reshape(batch, seq, d)
```

```python
import functools
import math

import jax
import jax.numpy as jnp
from jax import lax
from jax.experimental import pallas as pl
from jax.experimental.pallas import tpu as pltpu

F32 = jnp.float32
BF16 = jnp.bfloat16

HEAD_DIM = 64
LANES = 128
NORM_EPS = 1e-6
GN_EPS = 64e-5
DILATIONS = (1, 4, 16)
BAND = 128
QKV_W = 3 * LANES
INPROJ_TM = BAND * DILATIONS[1]
ATTN_GROUP = 1
LORA_DECAY, LORA_AAA, LORA_GATE, LORA_MV = 32, 32, 96, 32
LORA_PAD = 256
MIX_TM = 1024
MIX_SUB = 256
CHUNK = 64
RWKV_TC = 1024
RWKV_SUB = 256
NEG_BIG = -1e30
VMEM_LIMIT = 56 * 1024 * 1024


def _nt(a, b):
    return lax.dot_general(a, b, (((1,), (1,)), ((), ())), preferred_element_type=F32)


def _tn(a, b):
    return lax.dot_general(a, b, (((0,), (0,)), ((), ())), preferred_element_type=F32)


def _mm(a, b):
    return jnp.dot(a, b, preferred_element_type=F32)


def _rms(x, g):
    return x * lax.rsqrt(jnp.mean(x * x, axis=-1, keepdims=True) + NORM_EPS) * g


def _inproj_kernel(x_ref, g_ref, w_ref, mu_ref, x1_ref, x4_ref, x16_ref, zr_ref, zl_ref,
                   carry_ref, zs_ref, z4_ref, *, seq_tiles, n_pairs, d_rwkv3):
    i = pl.program_id(0)
    tm = x_ref.shape[0]
    h = _rms(x_ref[...], g_ref[...]).astype(BF16)
    d_attn3 = n_pairs * QKV_W
    f = DILATIONS[1]

    def attn_pairs(p0):
        zz = _mm(h, w_ref[:, p0 * QKV_W:(p0 + 2) * QKV_W])
        for p in (p0, p0 + 1):
            z = zz[:, (p - p0) * QKV_W:(p - p0 + 1) * QKV_W]
            x1_ref[p, :, 0] = z.astype(BF16).reshape(tm // BAND, BAND, QKV_W)
            for j in range(QKV_W // LANES):
                ls = slice(j * LANES, (j + 1) * LANES)
                zs_ref[p, j] = z[:, ls]
                for b in range(f):
                    z4 = zs_ref[p, j, pl.ds(b, tm // f, stride=f), :]
                    x4_ref[p, b, :, ls] = z4.astype(BF16)
                    z4_ref[p, j, b] = z4
                    for a in range(f):
                        z16 = z4_ref[p, j, b, pl.ds(a, tm // (f * f), stride=f), :]
                        x16_ref[p, f * a + b, :, ls] = z16.astype(BF16)

    first_tile = (i % seq_tiles) == 0
    row0 = lax.broadcasted_iota(jnp.int32, (tm, 1), 0) == 0

    def shifted(c0, w):
        z = _mm(h, w_ref[:, d_attn3 + c0:d_attn3 + c0 + w])
        carry = jnp.where(first_tile, 0.0, carry_ref[:, c0:c0 + w])
        prev = jnp.where(row0, carry, pltpu.roll(z, 1, 0))
        carry_ref[:, c0:c0 + w] = z[tm - 1:tm, :]
        zs = z + (prev - z) * mu_ref[:, c0:c0 + w]
        if c0 < d_rwkv3:
            zr_ref[:, c0:c0 + w] = zs
        else:
            zl_ref[...] = zs

    n_shift = d_rwkv3 + LORA_PAD
    cw = 512
    shift_tasks = [(c0, min(cw, n_shift - c0)) for c0 in range(0, n_shift, cw)]
    attn_tasks = list(range(0, n_pairs, 2))
    while attn_tasks or shift_tasks:
        if attn_tasks:
            attn_pairs(attn_tasks.pop(0))
        for _ in range(2):
            if shift_tasks:
                shifted(*shift_tasks.pop(0))


def _inproj(x2, g, w, mu, *, batch, seq):
    tm = INPROJ_TM
    t, d = x2.shape
    nc = w.shape[1]
    n_pairs = (d // 2) // LANES
    d_attn3 = n_pairs * QKV_W
    d_rwkv3 = nc - d_attn3 - LORA_PAD
    seq_tiles = seq // tm
    d4, d16 = DILATIONS[1], DILATIONS[2]
    assert tm == BAND * d4 and d16 == d4 * d4 and seq % (BAND * d16) == 0
    sub16 = BAND * d16 // tm
    kern = functools.partial(_inproj_kernel, seq_tiles=seq_tiles, n_pairs=n_pairs, d_rwkv3=d_rwkv3)
    return pl.pallas_call(
        kern,
        grid=(t // tm,),
        in_specs=[
            pl.BlockSpec((tm, d), lambda i: (i, 0)),
            pl.BlockSpec((1, d), lambda i: (0, 0)),
            pl.BlockSpec((d, nc), lambda i: (0, 0)),
            pl.BlockSpec((1, nc - d_attn3), lambda i: (0, 0)),
        ],
        out_specs=[
            pl.BlockSpec((n_pairs, None, tm // BAND, 1, BAND, QKV_W),
                         lambda i: (0, i // seq_tiles, i % seq_tiles, 0, 0, 0)),
            pl.BlockSpec((n_pairs, None, None, d4, BAND, QKV_W),
                         lambda i: (0, i // seq_tiles, i % seq_tiles, 0, 0, 0)),
            pl.BlockSpec((n_pairs, None, None, d16, tm // d16, QKV_W),
                         lambda i: (0, i // seq_tiles, (i % seq_tiles) // sub16, 0,
                                    (i % seq_tiles) % sub16, 0)),
            pl.BlockSpec((tm, d_rwkv3), lambda i: (i, 0)),
            pl.BlockSpec((tm, LORA_PAD), lambda i: (i, 0)),
        ],
        out_shape=[
            jax.ShapeDtypeStruct((n_pairs, batch, seq // BAND, 1, BAND, QKV_W), BF16),
            jax.ShapeDtypeStruct((n_pairs, batch, seq // (BAND * d4), d4, BAND, QKV_W), BF16),
            jax.ShapeDtypeStruct((n_pairs, batch, seq // (BAND * d16), d16, BAND, QKV_W), BF16),
            jax.ShapeDtypeStruct((t, d_rwkv3), F32),
            jax.ShapeDtypeStruct((t, LORA_PAD), F32),
        ],
        scratch_shapes=[pltpu.VMEM((1, nc - d_attn3), F32),
                        pltpu.VMEM((n_pairs, QKV_W // LANES, tm, LANES), F32),
                        pltpu.VMEM((n_pairs, QKV_W // LANES, d4, tm // d4, LANES), F32)],
        compiler_params=pltpu.CompilerParams(
            dimension_semantics=("arbitrary",), vmem_limit_bytes=VMEM_LIMIT),
        name="inproj",
    )(x2, g, w, mu)


def _attn_scores(x_ref, probs, head0):
    scale = 1.0 / math.sqrt(HEAD_DIM)
    out = []
    for r, n in probs:
        prev = jnp.maximum(n - 1, 0)
        q = x_ref[n, r, :, 0:LANES] * scale
        kw = jnp.concatenate([x_ref[prev, r, :, LANES:2 * LANES], x_ref[n, r, :, LANES:2 * LANES]],
                             axis=0)
        out.append(_nt(_block_diag(q, head0), kw))
    return out


def _attn_softmax(probs, scores, consts):
    head0, bias_first, bias_rest = consts
    s = []
    for (r, n), si in zip(probs, scores):
        bias = jnp.where(n == 0, bias_first, bias_rest)
        s.append(jnp.concatenate([si[:BAND] + bias, si[BAND:] + bias], axis=0))
    m = [jnp.max(si, axis=-1, keepdims=True) for si in s]
    p = [jnp.exp(si - mi).astype(BF16) for si, mi in zip(s, m)]
    return p, [jnp.where(head0, mi[:BAND], mi[BAND:]) for mi in m]


def _attn_pv(x_ref, probs, p, head0):
    ones = jnp.ones((2 * BAND, LANES), BF16)
    pv = []
    for (r, n), pi in zip(probs, p):
        prev = jnp.maximum(n - 1, 0)
        vw = jnp.concatenate([x_ref[prev, r, :, 2 * LANES:], x_ref[n, r, :, 2 * LANES:]], axis=0)
        pv.append(_mm(pi, jnp.concatenate([vw, ones], axis=1)))
    return [tuple(jnp.where(head0, t[:BAND], t[BAND:]) for t in (pvi[:, :LANES], pvi[:, LANES:]))
            for pvi in pv]


def _attn_kernel(x1_ref, x4_ref, x16_ref, out_ref, nat_s, ph_s, mid_s, p_scr):
    seq = out_ref.shape[0]
    lane = lax.broadcasted_iota(jnp.int32, (1, LANES), 1)
    head0 = lane < HEAD_DIM
    qrow = lax.broadcasted_iota(jnp.int32, (BAND, 1), 0)
    kcol = lax.broadcasted_iota(jnp.int32, (1, 2 * BAND), 1)
    rel = BAND + qrow - kcol
    bias_rest = jnp.where((rel >= 0) & (rel <= BAND), 0.0, NEG_BIG).astype(F32)
    bias_first = jnp.where(kcol < BAND, NEG_BIG, bias_rest)
    consts = (head0, bias_first, bias_rest)
    group = ATTN_GROUP
    n_groups = seq // (BAND * group)

    for bi, (d, x_ref) in reversed(list(enumerate(zip(DILATIONS, (x1_ref, x4_ref, x16_ref))))):
        n_blocks = seq // (BAND * d)
        shift = n_blocks.bit_length() - 1
        sub_len = seq // d
        dst = nat_s.at[bi] if d == 1 else ph_s

        def probs_of(g, shift=shift, n_blocks=n_blocks):
            js = [g * group + i for i in range(group)]
            return [(j >> shift, j & (n_blocks - 1)) for j in js], js

        def store(js, q, vals, dst=dst):
            for j, t in zip(js, vals):
                dst[q, pl.ds(pl.multiple_of(j * BAND, BAND), BAND), :] = t

        probs0, js0 = probs_of(jnp.int32(0))
        p0, m0 = _attn_softmax(probs0, _attn_scores(x_ref, probs0, head0), consts)
        store(js0, 1, m0)
        for i, pi in enumerate(p0):
            p_scr[i] = pi

        def body(g, carry, x_ref=x_ref, probs_of=probs_of, store=store):
            nxt, js_next = probs_of(jnp.minimum(g + 1, n_groups - 1))
            s_next = _attn_scores(x_ref, nxt, head0)
            cur, js = probs_of(g)
            acc_l = _attn_pv(x_ref, cur, [p_scr[i] for i in range(group)], head0)
            p_next, m_next = _attn_softmax(nxt, s_next, consts)
            store(js, 0, [t[0] for t in acc_l])
            store(js, 2, [t[1] for t in acc_l])
            store(js_next, 1, m_next)
            for i, pi in enumerate(p_next):
                p_scr[i] = pi
            return carry

        lax.fori_loop(0, n_groups, body, 0, unroll=True)
        f = DILATIONS[1]
        if d == f:
            for r in range(d):
                staged = slice(r * sub_len, (r + 1) * sub_len)
                for q in range(3):
                    nat_s[bi, q, pl.ds(r, sub_len, stride=f), :] = ph_s[q, staged, :]
        elif d == f * f:
            for q in range(3):
                for b in range(f):
                    for a in range(f):
                        staged = slice((f * a + b) * sub_len, (f * a + b + 1) * sub_len)
                        mid_s[pl.ds(b * f * sub_len + a, sub_len, stride=f), :] = ph_s[q, staged, :]
                for b in range(f):
                    rows = slice(b * f * sub_len, (b + 1) * f * sub_len)
                    nat_s[bi, q, pl.ds(b, f * sub_len, stride=f), :] = mid_s[rows, :]

    rows_c = 512
    for c in range(seq // rows_c):
        rows = slice(c * rows_c, (c + 1) * rows_c)
        ms = [nat_s[bi, 1, rows, :] for bi in range(3)]
        top = jnp.maximum(jnp.maximum(ms[0], ms[1]), ms[2])
        es = [jnp.exp(mb - top) for mb in ms]
        num = sum(e * nat_s[bi, 0, rows, :] for bi, e in enumerate(es))
        den = sum(e * nat_s[bi, 2, rows, :] for bi, e in enumerate(es))
        out_ref[rows, :] = (num / den).astype(out_ref.dtype)


def _attention(x1, x4, x16, *, batch, seq):
    n_pairs = x1.shape[0]
    spec = lambda x: pl.BlockSpec((None, None) + x.shape[2:], lambda b, p: (p, b, 0, 0, 0, 0))
    return pl.pallas_call(
        _attn_kernel,
        grid=(batch, n_pairs),
        in_specs=[spec(x1), spec(x4), spec(x16)],
        out_specs=pl.BlockSpec((None, seq, LANES), lambda b, p: (p, b, 0)),
        out_shape=jax.ShapeDtypeStruct((n_pairs, batch * seq, LANES), BF16),
        scratch_shapes=[pltpu.VMEM((3, 3, seq, LANES), F32), pltpu.VMEM((3, seq, LANES), F32),
                        pltpu.VMEM((seq, LANES), F32),
                        pltpu.VMEM((ATTN_GROUP, 2 * BAND, 2 * BAND), BF16)],
        compiler_params=pltpu.CompilerParams(
            dimension_semantics=("arbitrary", "arbitrary"), vmem_limit_bytes=VMEM_LIMIT),
        name="attention",
    )(x1, x4, x16)


def _block_diag(x, head0):
    zero = jnp.zeros_like(x)
    return jnp.concatenate([jnp.where(head0, x, zero), jnp.where(head0, zero, x)], axis=0)


def _chunk_consts():
    c = CHUNK
    trow = lax.broadcasted_iota(jnp.int32, (c, 1), 0)
    col3 = lax.broadcasted_iota(jnp.int32, (1, 3 * c), 1)
    tri3 = jnp.where((col3 & (c - 1)) <= trow, 1.0, 0.0).astype(BF16)
    lane_w = lax.broadcasted_iota(jnp.int32, (1, LANES), 1)
    scol = lane_w & (HEAD_DIM - 1)
    strict = scol < trow
    incl = scol <= trow
    eye_w = jnp.where(scol == trow, 1.0, 0.0).astype(F32)
    head0 = lane_w < HEAD_DIM
    rr = lax.broadcasted_iota(jnp.int32, (LANES, 1), 0)
    same_head = (rr // HEAD_DIM) == (lane_w // HEAD_DIM)
    levels = []
    s = 1
    while s < c:
        in_pair = (trow // (2 * s)) == (scol // (2 * s))
        levels.append(in_pair & ((trow // s) != (scol // s)) & strict)
        s *= 2
    return dict(tri3=tri3, strict=strict, incl=incl, eye_w=eye_w, head0=head0,
                same_head=same_head, levels=levels)


def _chunk_precompute(ins, consts):
    c = CHUNK
    head0 = consts["head0"]
    strict, incl, same_head = consts["strict"], consts["incl"], consts["same_head"]
    levels = consts["levels"]
    zero = jnp.zeros((c, LANES), F32)
    zero_sq = jnp.zeros((LANES, LANES), F32)

    def bd(x):
        return _block_diag(x, head0).astype(BF16)

    splits = []
    for (_, lw, _, _, _, _) in ins:
        hi = lw.astype(BF16)
        r1 = lw - hi.astype(F32)
        mid = r1.astype(BF16)
        lo = (r1 - mid.astype(F32)).astype(BF16)
        splits.append(jnp.concatenate([hi, mid, lo], axis=0))
    cums = []
    for i in range(0, len(splits), 2):
        both = _mm(consts["tri3"], jnp.concatenate(splits[i:i + 2], axis=1))
        cums += [both[:, :LANES], both[:, LANES:]]
    yield
    ops = []
    for (r, lw, k, v, kk, b), cum in zip(ins, cums):
        cum_end = cum[c - 1:c, :]
        e_neg = jnp.exp(-cum)
        e_end = jnp.exp(cum_end - cum)
        ops.append(dict(kq=kk * jnp.exp(cum - lw), rq=r * jnp.exp(cum), kt=k * e_neg, bt=b * e_neg,
                        kh=(k * e_end).astype(BF16), bh=(b * e_end).astype(BF16),
                        g=jnp.exp(cum_end), v=v))
    a_all = [_nt(jnp.concatenate([o["kq"], o["rq"]], axis=0).astype(BF16),
                 jnp.concatenate([bd(o["bt"]), bd(o["kt"])], axis=0)) for o in ops]
    a_ab = [jnp.where(strict, a[:c, :LANES], zero) for a in a_all]
    a_rb = [jnp.where(incl, a[c:, :LANES], zero).astype(BF16) for a in a_all]
    av = [_mm(jnp.concatenate([jnp.where(strict, a[:c, LANES:], zero),
                               jnp.where(incl, a[c:, LANES:], zero)], axis=0).astype(BF16),
              bd(o["v"])) for a, o in zip(a_all, ops)]
    yield
    t_inv = [consts["eye_w"] - jnp.where(levels[0], a, zero) for a in a_ab]
    for mask in levels[1:]:
        x = [_mm(jnp.where(mask, a, zero).astype(BF16), bd(t)) for a, t in zip(a_ab, t_inv)]
        yield
        t_inv = [t - _mm(t.astype(BF16), bd(xi)) for t, xi in zip(t_inv, x)]
        yield
    twu = [_mm(t.astype(BF16), jnp.concatenate([bd(o["kq"]), bd(a[:c])], axis=1))
           for t, o, a in zip(t_inv, ops, av)]
    yield
    arb = [_mm(ar, jnp.concatenate([bd(x[:, :LANES]), bd(x[:, LANES:])], axis=1))
           for ar, x in zip(a_rb, twu)]
    yield
    out = []
    for o, a, x, y in zip(ops, av, twu, arb):
        w = x[:, :LANES].astype(BF16)
        u0 = x[:, LANES:]
        pm = -jnp.where(same_head, _tn(w, o["bh"]), zero_sq)
        dd = jnp.where(same_head,
                       _tn(jnp.concatenate([o["v"], -u0], axis=0).astype(BF16),
                           jnp.concatenate([o["kh"], o["bh"]], axis=0)), zero_sq)
        out.append((o["rq"] - y[:, :LANES], a[c:] - y[:, LANES:], pm, dd, o["g"]))
    return out


def _spread(fillers, every=2):
    out = []
    for f in fillers:
        out.append(f)
        out.extend([lambda: None] * (every - 1))
    return out


def _interleave(gen, fillers):
    fillers = list(fillers)
    while True:
        try:
            next(gen)
        except StopIteration as stop:
            result = stop.value
            break
        if fillers:
            fillers.pop(0)()
    for f in fillers:
        f()
    return result


def _chunk_apply(pre, states):
    sb = [s.astype(BF16) for s in states]
    ys = [_nt(p[0].astype(BF16), s) + p[1] for p, s in zip(pre, sb)]
    new = [st * p[4] + _mm(s, p[2].astype(BF16)) + p[3] for p, s, st in zip(pre, sb, states)]
    return ys, new


def _rwkv_kernel(*refs, first_layer, d_rwkv):
    if first_layer:
        (zr_ref, zl_ref, wl_ref, vec_ref, out_ref, vfirst_out_ref,
         r_s, lw_s, k_s, v_s, kk_s, b_s, g_s, y_s, state_s) = refs
        vfirst_ref = None
    else:
        (zr_ref, zl_ref, vfirst_ref, wl_ref, vec_ref, out_ref,
         r_s, lw_s, k_s, v_s, kk_s, b_s, g_s, y_s, state_s) = refs
    tc = zr_ref.shape[0]
    n_pairs = d_rwkv // LANES
    c = CHUNK
    sub = RWKV_SUB
    sub_chunks = sub // c

    @pl.when(pl.program_id(1) == 0)
    def _():
        state_s[...] = jnp.zeros_like(state_s)

    w0, a0, mv0, k_k, k_a, r_k, gn_w, gn_b = (vec_ref[j:j + 1, :] for j in range(8))
    hrow = lax.broadcasted_iota(jnp.int32, (2 * LANES, 1), 0)
    hcol = lax.broadcasted_iota(jnp.int32, (1, 2 * LANES), 1)
    same = jnp.where((hrow // HEAD_DIM) == (hcol // HEAD_DIM), 1.0, 0.0).astype(BF16)

    def head_sum(x):
        xb = x.astype(BF16)
        w = 2 * LANES
        return jnp.concatenate([_mm(xb[:, j * w:(j + 1) * w], same) for j in range(d_rwkv // w)],
                               axis=1)

    lane = lax.broadcasted_iota(jnp.int32, (1, LORA_PAD), 1)
    o1 = LORA_DECAY
    o2 = o1 + LORA_AAA
    o3 = o2 + LORA_GATE

    def prepare(rows):
        zl = zl_ref[rows, :]
        act = jnp.where(lane < o1, jnp.tanh(zl),
                        jnp.where((lane >= o2) & (lane < o3), jax.nn.sigmoid(zl), zl))
        lora = _mm(act.astype(BF16), wl_ref[...])
        r = zr_ref[rows, 0:d_rwkv]
        k = zr_ref[rows, d_rwkv:2 * d_rwkv]
        v = zr_ref[rows, 2 * d_rwkv:3 * d_rwkv]
        lw = -math.exp(-0.5) * jax.nn.sigmoid(w0 + lora[:, 0:d_rwkv])
        a = jax.nn.sigmoid(a0 + lora[:, d_rwkv:2 * d_rwkv])
        if first_layer:
            vfirst_out_ref[rows, :] = v
        else:
            vgate = jax.nn.sigmoid(mv0 + lora[:, 3 * d_rwkv:4 * d_rwkv])
            v = v + (vfirst_ref[rows, :] - v) * vgate
        kk = k * k_k
        kk = kk / jnp.maximum(jnp.sqrt(head_sum(kk * kk)), 1e-12)
        r_s[rows, :] = r
        lw_s[rows, :] = lw
        k_s[rows, :] = k * (1.0 + (a - 1.0) * k_a)
        v_s[rows, :] = v
        kk_s[rows, :] = kk
        b_s[rows, :] = kk * a
        g_s[rows, :] = lora[:, 2 * d_rwkv:3 * d_rwkv]

    def problems(h):
        ins = []
        for ci in range(h * sub_chunks, (h + 1) * sub_chunks):
            rows = slice(ci * c, (ci + 1) * c)
            for p in range(n_pairs):
                ls = slice(p * LANES, (p + 1) * LANES)
                ins.append(tuple(ref[rows, ls] for ref in (r_s, lw_s, k_s, v_s, kk_s, b_s)))
        return ins

    states = [state_s[p] for p in range(n_pairs)]

    def chain_steps(h, pre):
        def step(j):
            def run():
                ys, new = _chunk_apply(pre[j * n_pairs:(j + 1) * n_pairs], states)
                states[:] = new
                ci = h * sub_chunks + j
                for p in range(n_pairs):
                    y_s[ci * c:(ci + 1) * c, p * LANES:(p + 1) * LANES] = ys[p]
            return run

        rows = slice(h * sub, (h + 1) * sub)
        tmp = {}

        def stats():
            y = y_s[rows, :]
            tmp["dlt"] = y - head_sum(y) * (1.0 / HEAD_DIM)
            tmp["bonus"] = head_sum(r_s[rows, :] * k_s[rows, :] * r_k) * v_s[rows, :]

        def finish():
            dlt = tmp["dlt"]
            var = head_sum(dlt * dlt) * (1.0 / HEAD_DIM)
            yn = dlt * lax.rsqrt(var + GN_EPS) * gn_w + gn_b
            out_ref[rows, :] = ((yn + tmp["bonus"]) * g_s[rows, :]).astype(out_ref.dtype)

        return [step(j) for j in range(sub_chunks)] + [stats, finish]

    consts = _chunk_consts()
    n_sub = tc // sub
    for h in range(n_sub):
        prepare(slice(h * sub, (h + 1) * sub))
    pending = []
    for h in range(n_sub):
        pre = _interleave(_chunk_precompute(problems(h), consts), _spread(pending))
        pending = chain_steps(h, pre)
    for f in pending:
        f()
    for p in range(n_pairs):
        state_s[p] = states[p]


def _rwkv(zr, zl, vfirst, wl, vecs, *, batch, seq, tc=RWKV_TC):
    t = zr.shape[0]
    d_rwkv = zr.shape[1] // 3
    first_layer = vfirst is None
    tiles = seq // tc
    row = lambda b, i: (b * tiles + i, 0)
    const = lambda b, i: (0, 0)
    in_specs = [pl.BlockSpec((tc, 3 * d_rwkv), row), pl.BlockSpec((tc, LORA_PAD), row)]
    args = [zr, zl]
    if not first_layer:
        in_specs.append(pl.BlockSpec((tc, d_rwkv), row))
        args.append(vfirst)
    in_specs += [pl.BlockSpec(wl.shape, const), pl.BlockSpec(vecs.shape, const)]
    args += [wl, vecs]
    out_specs = [pl.BlockSpec((tc, d_rwkv), row)]
    out_shape = [jax.ShapeDtypeStruct((t, d_rwkv), BF16)]
    if first_layer:
        out_specs.append(pl.BlockSpec((tc, d_rwkv), row))
        out_shape.append(jax.ShapeDtypeStruct((t, d_rwkv), F32))
    kern = functools.partial(_rwkv_kernel, first_layer=first_layer, d_rwkv=d_rwkv)
    res = pl.pallas_call(
        kern,
        grid=(batch, tiles),
        in_specs=in_specs,
        out_specs=out_specs,
        out_shape=out_shape,
        scratch_shapes=[pltpu.VMEM((tc, d_rwkv), F32) for _ in range(8)]
        + [pltpu.VMEM((d_rwkv // LANES, LANES, LANES), F32)],
        compiler_params=pltpu.CompilerParams(
            dimension_semantics=("arbitrary", "arbitrary"), vmem_limit_bytes=VMEM_LIMIT),
        name="rwkv7_first" if first_layer else "rwkv7",
    )(*args)
    return (res[0], res[1]) if first_layer else (res[0], vfirst)


def _mix_ffn_kernel(a_ref, rw_ref, x_ref, gains_ref, ga_ref, wo_ref, wu_ref, wd_ref, out_ref,
                    acc_ref, *, ff_chunk):
    n_pairs = a_ref.shape[0]
    tm = x_ref.shape[0]
    g_post, g_pre, g_ffn = (gains_ref[j:j + 1, :] for j in range(3))
    d_attn = n_pairs * LANES
    d_ff = wu_ref.shape[1]
    halves = [slice(j * MIX_SUB, (j + 1) * MIX_SUB) for j in range(tm // MIX_SUB)]
    x1s, hs = [], []
    for rows in halves:
        attn = jnp.concatenate([a_ref[p, rows, :].astype(F32) for p in range(n_pairs)], axis=-1)
        attn = _rms(attn, ga_ref[...]).astype(BF16)
        mixed = _mm(attn, wo_ref[0:d_attn, :]) + _mm(rw_ref[rows, :], wo_ref[d_attn:, :])
        x1 = x_ref[rows, :] + _rms(mixed, g_post)
        x1s.append(x1)
        hs.append(_rms(x1, g_pre).astype(BF16))
    for rows, h in zip(halves, hs):
        for c in range(d_ff // ff_chunk):
            cs = slice(c * ff_chunk, (c + 1) * ff_chunk)
            u = jnp.maximum(_mm(h, wu_ref[:, cs]), 0.0)
            part = _mm((u * u).astype(BF16), wd_ref[cs, :])
            if c == 0:
                acc_ref[rows, :] = part
            else:
                acc_ref[rows, :] += part
    for rows, x1 in zip(halves, x1s):
        out_ref[rows, :] = x1 + _rms(acc_ref[rows, :], g_ffn)


def _mix_ffn(attn, rw, x2, gains, ga, wo, wu, wd, *, layer, tm=MIX_TM, ff_chunk=1024):
    t, d = x2.shape
    n_pairs = attn.shape[0]
    d_rwkv = rw.shape[1]
    row = lambda i: (i, 0)
    const = lambda i: (0, 0)
    resident = lambda shape: pl.BlockSpec((None,) + shape[1:], lambda i: (layer, 0, 0),
                                          pipeline_mode=pl.Buffered(1))
    return pl.pallas_call(
        functools.partial(_mix_ffn_kernel, ff_chunk=ff_chunk),
        grid=(t // tm,),
        in_specs=[pl.BlockSpec((n_pairs, tm, LANES), lambda i: (0, i, 0)),
                  pl.BlockSpec((tm, d_rwkv), row), pl.BlockSpec((tm, d), row),
                  pl.BlockSpec(gains.shape, const), pl.BlockSpec(ga.shape, const),
                  resident(wo.shape), resident(wu.shape), resident(wd.shape)],
        out_specs=pl.BlockSpec((tm, d), row),
        out_shape=jax.ShapeDtypeStruct((t, d), F32),
        scratch_shapes=[pltpu.VMEM((tm, d), F32)],
        compiler_params=pltpu.CompilerParams(
            dimension_semantics=("arbitrary",), vmem_limit_bytes=VMEM_LIMIT,
            allow_input_fusion=[False] * 5 + [True] * 3),
        name="mix_ffn",
    )(attn, rw, x2, gains, ga, wo, wu, wd)


def _layer_params(i, d_attn, d_rwkv, w_in_first, w_in_rest, mu_shift, mu_shift_mv,
                  decay_up, aaa_up, gate_up, mv_up):
    w = w_in_first if i == 0 else w_in_rest[i - 1]
    d = w.shape[0]
    n_first = w_in_first.shape[1]
    lora_used = LORA_DECAY + LORA_AAA + LORA_GATE + LORA_MV
    n_pairs = d_attn // LANES
    qkv = [w[:, j * d_attn + p * LANES:j * d_attn + (p + 1) * LANES]
           for p in range(n_pairs) for j in range(3)]
    pieces = qkv + [w[:, 3 * d_attn:n_first]]
    mus = [mu_shift[i]]
    if i == 0:
        pieces.append(jnp.zeros((d, LORA_MV), F32))
        mus.append(jnp.zeros((LORA_MV,), F32))
    else:
        pieces.append(w[:, n_first:])
        mus.append(mu_shift_mv[i - 1])
    pieces.append(jnp.zeros((d, LORA_PAD - lora_used), F32))
    mus.append(jnp.zeros((LORA_PAD - lora_used,), F32))
    w_r = jnp.concatenate(pieces, axis=1).astype(BF16)
    mu_r = jnp.concatenate(mus)[None, :]
    wl = jnp.zeros((LORA_PAD, 4 * d_rwkv), F32)
    o1 = LORA_DECAY
    o2 = o1 + LORA_AAA
    o3 = o2 + LORA_GATE
    wl = wl.at[0:o1, 0:d_rwkv].set(decay_up[i])
    wl = wl.at[o1:o2, d_rwkv:2 * d_rwkv].set(aaa_up[i])
    wl = wl.at[o2:o3, 2 * d_rwkv:3 * d_rwkv].set(gate_up[i])
    if i > 0:
        wl = wl.at[o3:o3 + LORA_MV, 3 * d_rwkv:4 * d_rwkv].set(mv_up[i - 1])
    return w_r, mu_r, wl.astype(BF16)


def kernel(x, norm_mix_pre, norm_mix_post, norm_ffn_pre, norm_ffn_post, w_in_first, w_in_rest,
           mu_shift, mu_shift_mv, attn_out_gain, decay_w0, decay_up, aaa_a0, aaa_up, mv_v0, mv_up,
           gate_up, k_k, k_a, r_k, gn_w, gn_b, w_out, w_ffn_up, w_ffn_down):
    batch, seq, d = x.shape
    depth = norm_mix_pre.shape[0]
    d_attn = attn_out_gain.shape[1]
    d_rwkv = decay_w0.shape[1]
    x2 = x.reshape(batch * seq, d)
    wo_all, wu_all, wd_all = (w.astype(BF16) for w in (w_out, w_ffn_up, w_ffn_down))
    vfirst = None
    for i in range(depth):
        w_r, mu_r, wl = _layer_params(i, d_attn, d_rwkv, w_in_first, w_in_rest, mu_shift,
                                      mu_shift_mv, decay_up, aaa_up, gate_up, mv_up)
        mv0 = mv_v0[i - 1] if i > 0 else jnp.zeros((d_rwkv,), F32)
        vecs = jnp.stack([decay_w0[i], aaa_a0[i], mv0, k_k[i], k_a[i], r_k[i].reshape(-1),
                          gn_w[i], gn_b[i]])
        x1, x4, x16, zr, zl = _inproj(x2, norm_mix_pre[i][None, :], w_r, mu_r, batch=batch, seq=seq)
        attn = _attention(x1, x4, x16, batch=batch, seq=seq)
        rw, vfirst = _rwkv(zr, zl, vfirst, wl, vecs, batch=batch, seq=seq)
        gains = jnp.stack([norm_mix_post[i], norm_ffn_pre[i], norm_ffn_post[i]])
        x2 = _mix_ffn(attn, rw, x2, gains, attn_out_gain[i][None, :], wo_all, wu_all, wd_all,
                      layer=i)
    return x2.reshape(batch, seq, d)
```
